```python
import jax
import jax.numpy as jnp
from jax import lax
import numpy as np

D_MODEL = 1024
BATCH = 4
SEQ = 4096
DEPTH = 1
DEC_BATCH = 128
DEC_SEQ = 4
PAST_LEN = 8192
PAGE_SIZE = 128

N_HEADS_MOBA = 8
HEAD_DIM_MOBA = 64
W_MOBA = N_HEADS_MOBA * HEAD_DIM_MOBA
MOBA_BLOCK = 256
MOBA_TOPK = 3
N_HEADS_SB = 8
HEAD_DIM_SB = 64
W_SB = N_HEADS_SB * HEAD_DIM_SB
Q_BLOCK = 128
N_MEM = 256
N_HEADS_MEM = 4
HEAD_DIM_MEM = 128
W_MEM = N_HEADS_MEM * HEAD_DIM_MEM
PEER_HEADS = 8
PEER_KEYS = 128
PEER_EXPERTS = PEER_KEYS * PEER_KEYS
PEER_DKEY = 256
PEER_TOPK = 16
PEER_TOKEN_BLOCK = 128
RMS_EPS = 1e-6
IN_SPLITS = (W_MOBA, W_MOBA, W_MOBA, W_SB, W_SB, W_SB, D_MODEL, D_MODEL)
F32 = jnp.float32

kernel_name = "hybrid_moba_stickbreak_peer_step"


def rmsnorm(x, g):
    xf = x.astype(F32)
    y = xf * lax.rsqrt(jnp.mean(xf * xf, axis=-1, keepdims=True) + RMS_EPS)
    return (y * g).astype(x.dtype)


def alibi_slopes():
    return 2.0 ** (-(8.0 / N_HEADS_MOBA) * jnp.arange(1, N_HEADS_MOBA + 1, dtype=F32))


def in_proj(h, w_in):
    B, T, _ = h.shape
    z = h @ w_in
    offs = [int(o) for o in np.cumsum(IN_SPLITS)[:-1]]
    qa, ka, va, qb, kb, vb, ga, gb = jnp.split(z, offs, axis=-1)
    ra = lambda a: a.reshape(B, T, N_HEADS_MOBA, HEAD_DIM_MOBA)
    rb = lambda a: a.reshape(B, T, N_HEADS_SB, HEAD_DIM_SB)
    return ra(qa), ra(ka), ra(va), rb(qb), rb(kb), rb(vb), ga, gb


def merge_branches(oa, ob, ga, gb, w_br_moba, w_br_sb, w_out):
    B, T = oa.shape[:2]
    ya = oa.reshape(B, T, W_MOBA) @ w_br_moba
    yb = ob.reshape(B, T, W_SB) @ w_br_sb
    return (jax.nn.sigmoid(ga) * ya + jax.nn.sigmoid(gb) * yb) @ w_out


def moba_attend(q, k_sel, v_sel, sel_valid, sel_dist, k_own, v_own, own_valid, own_dist, slope):
    scale = q.shape[-1] ** -0.5
    lo = jnp.einsum('bqd,bkd->bqk', q, k_own, preferred_element_type=F32) * scale - slope * own_dist
    lo = jnp.where(own_valid, lo, -jnp.inf)
    if k_sel is None:
        p = jax.nn.softmax(lo, axis=-1)
        o = jnp.einsum('bqk,bkd->bqd', p, v_own, preferred_element_type=F32)
    else:
        B, Q, J, K = sel_dist.shape
        ls = jnp.einsum('bqd,bqjkd->bqjk', q, k_sel, preferred_element_type=F32) * scale - slope * sel_dist
        ls = jnp.where(sel_valid[..., None], ls, -jnp.inf).reshape(B, Q, J * K)
        p = jax.nn.softmax(jnp.concatenate([ls, lo], axis=-1), axis=-1)
        o = (jnp.einsum('bqjk,bqjkd->bqd', p[..., :J * K].reshape(B, Q, J, K), v_sel, preferred_element_type=F32)
             + jnp.einsum('bqk,bkd->bqd', p[..., J * K:], v_own, preferred_element_type=F32))
    return o.astype(q.dtype)


def moba_prompt(q, k, v, slopes):
    B, T, H, Dh = q.shape
    nb = -(-T // MOBA_BLOCK)
    pad = nb * MOBA_BLOCK - T
    kb = jnp.pad(k, ((0, 0), (0, pad), (0, 0), (0, 0))).reshape(B, nb, MOBA_BLOCK, H, Dh)
    vb = jnp.pad(v, ((0, 0), (0, pad), (0, 0), (0, 0))).reshape(B, nb, MOBA_BLOCK, H, Dh)
    kmean = jnp.mean(kb, axis=2, dtype=F32)
    n_sel = min(MOBA_TOPK, nb - 1)
    nqb = T // Q_BLOCK
    bidx = jnp.arange(B)[:, None, None]

    def per_head(args):
        qh, kbh, vbh, kmh, m = args

        def per_qblock(args2):
            qi, qq = args2
            t = qi * Q_BLOCK + jnp.arange(Q_BLOCK)
            own_b = (qi * Q_BLOCK) // MOBA_BLOCK
            k_own = lax.dynamic_index_in_dim(kbh, own_b, axis=1, keepdims=False)
            v_own = lax.dynamic_index_in_dim(vbh, own_b, axis=1, keepdims=False)
            own_pos = own_b * MOBA_BLOCK + jnp.arange(MOBA_BLOCK)
            own_valid = own_pos[None, :] <= t[:, None]
            own_dist = (t[:, None] - own_pos[None, :]).astype(F32)
            k_sel = v_sel = sel_valid = sel_dist = None
            if n_sel > 0:
                sc = jnp.einsum('bqd,bnd->bqn', qq, kmh, preferred_element_type=F32)
                sc = jnp.where(jnp.arange(nb) < own_b, sc, -jnp.inf)
                _, idx = lax.top_k(sc, n_sel)
                k_sel = kbh[bidx, idx]
                v_sel = vbh[bidx, idx]
                sel_valid = idx < own_b
                sel_dist = (t[None, :, None, None] - (idx[..., None] * MOBA_BLOCK + jnp.arange(MOBA_BLOCK))).astype(F32)
            return moba_attend(qq, k_sel, v_sel, sel_valid, sel_dist, k_own, v_own, own_valid, own_dist, m)

        qblocks = qh.reshape(B, nqb, Q_BLOCK, Dh).transpose(1, 0, 2, 3)
        ob = lax.map(per_qblock, (jnp.arange(nqb), qblocks))
        return ob.transpose(1, 0, 2, 3).reshape(B, T, Dh)

    out = lax.map(per_head, (q.transpose(2, 0, 1, 3), kb.transpose(3, 0, 1, 2, 4), vb.transpose(3, 0, 1, 2, 4),
                             kmean.transpose(2, 0, 1, 3), slopes))
    return out.transpose(1, 2, 0, 3)


def moba_sample(q, k_new, v_new, cache_k, cache_v, page_table, slopes):
    DB, Qn, H, Dh = q.shape
    n_pages = page_table.shape[1]
    past = n_pages * PAGE_SIZE
    ppb = MOBA_BLOCK // PAGE_SIZE
    nfp = past // MOBA_BLOCK
    own_b = nfp
    own_first_page = nfp * ppb
    n_sel = min(MOBA_TOPK, nfp)
    t = past + jnp.arange(Qn)
    own_pages = page_table[:, own_first_page:]
    n_own_rows = own_pages.shape[1] * PAGE_SIZE
    k_own = jnp.concatenate([cache_k[own_pages].reshape(DB, n_own_rows, H, Dh), k_new], axis=1)
    v_own = jnp.concatenate([cache_v[own_pages].reshape(DB, n_own_rows, H, Dh), v_new], axis=1)
    own_pos = jnp.concatenate([own_b * MOBA_BLOCK + jnp.arange(n_own_rows), t])
    own_valid = own_pos[None, :] <= t[:, None]
    own_dist = (t[:, None] - own_pos[None, :]).astype(F32)
    if n_sel > 0:
        psum = lax.map(lambda pcol: jnp.sum(cache_k[pcol], axis=1, dtype=F32), page_table[:, :own_first_page].T)
        kmean = psum.reshape(nfp, ppb, DB, H, Dh).sum(axis=1) / MOBA_BLOCK
        kmean_h = kmean.transpose(2, 1, 0, 3)
    else:
        kmean_h = jnp.zeros((H, DB, 0, Dh), F32)
    bidx = jnp.arange(DB)[:, None, None, None]

    def per_head(args):
        h, qh, koh, voh, kmh, m = args
        k_sel = v_sel = sel_valid = sel_dist = None
        if n_sel > 0:
            sc = jnp.einsum('bqd,bnd->bqn', qh, kmh, preferred_element_type=F32)
            _, idx = lax.top_k(sc, n_sel)
            phys = page_table[bidx, idx[..., None] * ppb + jnp.arange(ppb)]
            k_sel = cache_k[phys, :, h, :].reshape(DB, Qn, n_sel, MOBA_BLOCK, Dh)
            v_sel = cache_v[phys, :, h, :].reshape(DB, Qn, n_sel, MOBA_BLOCK, Dh)
            sel_valid = idx < own_b
            sel_dist = (t[None, :, None, None] - (idx[..., None] * MOBA_BLOCK + jnp.arange(MOBA_BLOCK))).astype(F32)
        return moba_attend(qh, k_sel, v_sel, sel_valid, sel_dist, koh, voh, own_valid, own_dist, m)

    out = lax.map(per_head, (jnp.arange(H), q.transpose(2, 0, 1, 3), k_own.transpose(2, 0, 1, 3),
                             v_own.transpose(2, 0, 1, 3), kmean_h, slopes))
    return out.transpose(1, 2, 0, 3)


def sb_partial(q, k, v, valid, log_surv):
    scale = q.shape[-1] ** -0.5
    z = jnp.einsum('bqhd,bkhd->bhqk', q, k, preferred_element_type=F32) * scale
    lf = jnp.where(valid, jax.nn.log_sigmoid(-z), 0.0)
    after = lax.cumsum(lf, axis=3, reverse=True) - lf
    log_a = jax.nn.log_sigmoid(z) + after + log_surv[..., None]
    a = jnp.where(valid, jnp.exp(log_a), 0.0)
    o = jnp.einsum('bhqk,bkhd->bqhd', a, v, preferred_element_type=F32)
    return o, log_surv + jnp.sum(lf, axis=-1)


def sb_prompt(q, k, v):
    B, T, H, Dh = q.shape
    nqb = T // Q_BLOCK
    s_pos = jnp.arange(T)

    def blk(args):
        qi, qq = args
        t = qi * Q_BLOCK + jnp.arange(Q_BLOCK)
        o, _ = sb_partial(qq, k, v, s_pos[None, :] < t[:, None], jnp.zeros((B, H, Q_BLOCK), F32))
        return o.astype(q.dtype)

    qblocks = q.reshape(B, nqb, Q_BLOCK, H, Dh).transpose(1, 0, 2, 3, 4)
    out = lax.map(blk, (jnp.arange(nqb), qblocks))
    return out.transpose(1, 0, 2, 3, 4).reshape(B, T, H, Dh)


def sb_sample(q, k_new, v_new, cache_k, cache_v, page_table):
    DB, Qn, H, Dh = q.shape
    i = jnp.arange(Qn)
    o, s = sb_partial(q, k_new, v_new, i[None, :] < i[:, None], jnp.zeros((DB, H, Qn), F32))
    all_valid = jnp.ones((Qn, PAGE_SIZE), dtype=bool)

    def step(carry, pcol):
        o_acc, s_acc = carry
        o_p, s_acc = sb_partial(q, cache_k[pcol], cache_v[pcol], all_valid, s_acc)
        return (o_acc + o_p, s_acc), None

    (o, _), _ = lax.scan(step, (o, s), page_table.T[::-1])
    return o.astype(q.dtype)


def mem_kv(mem, g, w):
    B, M, _ = mem.shape
    k, v = jnp.split(rmsnorm(mem, g) @ w, 2, axis=-1)
    return k.reshape(B, M, N_HEADS_MEM, HEAD_DIM_MEM), v.reshape(B, M, N_HEADS_MEM, HEAD_DIM_MEM)


def mem_attend(h, w_q, mk, mv, w_o):
    B, T, _ = h.shape
    q = (h @ w_q).reshape(B, T, N_HEADS_MEM, HEAD_DIM_MEM)
    logits = jnp.einsum('bthd,bmhd->bhtm', q, mk, preferred_element_type=F32) * (HEAD_DIM_MEM ** -0.5)
    p = jax.nn.softmax(logits, axis=-1)
    o = jnp.einsum('bhtm,bmhd->bthd', p, mv, preferred_element_type=F32).astype(h.dtype)
    return o.reshape(B, T, W_MEM) @ w_o


def peer_ffn(h, w_q, sub_keys, u_tab, v_tab):
    B, T, D = h.shape
    n = B * T
    nblk = -(-n // PEER_TOKEN_BLOCK)
    hf = jnp.pad(h.reshape(n, D), ((0, nblk * PEER_TOKEN_BLOCK - n), (0, 0))).reshape(nblk, PEER_TOKEN_BLOCK, D)

    def blk(hb):
        q = (hb @ w_q).reshape(PEER_TOKEN_BLOCK, PEER_HEADS, 2, PEER_DKEY // 2)
        s = jnp.einsum('nhcd,hckd->nhck', q, sub_keys, preferred_element_type=F32)
        s1, i1 = lax.top_k(s[:, :, 0], PEER_TOPK)
        s2, i2 = lax.top_k(s[:, :, 1], PEER_TOPK)
        cand = (s1[..., :, None] + s2[..., None, :]).reshape(PEER_TOKEN_BLOCK, PEER_HEADS, PEER_TOPK * PEER_TOPK)
        cidx = (i1[..., :, None] * PEER_KEYS + i2[..., None, :]).reshape(PEER_TOKEN_BLOCK, PEER_HEADS, PEER_TOPK * PEER_TOPK)
        top, pos = lax.top_k(cand, PEER_TOPK)
        eidx = jnp.take_along_axis(cidx, pos, axis=-1)
        g = jax.nn.softmax(top, axis=-1)
        a = jax.nn.gelu(jnp.einsum('nd,nhkd->nhk', hb, u_tab[eidx], preferred_element_type=F32), approximate=False)
        return jnp.einsum('nhk,nhkd->nd', g * a, v_tab[eidx], preferred_element_type=F32).astype(hb.dtype)

    out = lax.map(blk, hf).reshape(nblk * PEER_TOKEN_BLOCK, D)[:n]
    return out.reshape(B, T, D)


def setup_inputs(seed: int = 0) -> dict:
    key = jax.random.key(seed)
    ks = jax.random.split(key, 28)
    nrm = lambda k, shape, s: jax.random.normal(k, shape, F32) * s
    gain = lambda k: 1.0 + 0.01 * jax.random.normal(k, (D_MODEL,), F32)
    n_pages = PAST_LEN // PAGE_SIZE
    n_used = DEC_BATCH * n_pages
    n_phys = n_used + max(1, n_used // 4)
    page_table = jax.random.permutation(ks[0], n_phys)[:n_used].reshape(DEC_BATCH, n_pages).astype(jnp.int32)
    pool_a = (n_phys, PAGE_SIZE, N_HEADS_MOBA, HEAD_DIM_MOBA)
    pool_b = (n_phys, PAGE_SIZE, N_HEADS_SB, HEAD_DIM_SB)
    mem_shape = (DEC_BATCH, N_MEM, N_HEADS_MEM, HEAD_DIM_MEM)
    return {
        "x_prompt": nrm(ks[1], (BATCH, SEQ, D_MODEL), 1.0),
        "x_sample": nrm(ks[2], (DEC_BATCH, DEC_SEQ, D_MODEL), 1.0),
        "cache_mem_k": nrm(ks[3], mem_shape, 1.0),
        "cache_mem_v": nrm(ks[4], mem_shape, 1.0),
        "cache_moba_k": nrm(ks[5], pool_a, 1.0),
        "cache_moba_v": nrm(ks[6], pool_a, 1.0),
        "cache_sb_k": nrm(ks[7], pool_b, 1.0),
        "cache_sb_v": nrm(ks[8], pool_b, 1.0),
        "page_table": page_table,
        "mem_prompt": nrm(ks[9], (BATCH, N_MEM, D_MODEL), 1.0),
        "g_mix": gain(ks[10]),
        "w_in": nrm(ks[11], (D_MODEL, sum(IN_SPLITS)), D_MODEL ** -0.5),
        "w_br_moba": nrm(ks[12], (W_MOBA, D_MODEL), W_MOBA ** -0.5),
        "w_br_sb": nrm(ks[13], (W_SB, D_MODEL), W_SB ** -0.5),
        "w_out": nrm(ks[14], (D_MODEL, D_MODEL), D_MODEL ** -0.5),
        "g_mem_q": gain(ks[15]),
        "g_mem_kv": gain(ks[16]),
        "w_mem_q": nrm(ks[17], (D_MODEL, W_MEM), D_MODEL ** -0.5),
        "w_mem_kv": nrm(ks[18], (D_MODEL, 2 * W_MEM), D_MODEL ** -0.5),
        "w_mem_o": nrm(ks[19], (W_MEM, D_MODEL), W_MEM ** -0.5),
        "g_ffn": gain(ks[20]),
        "w_peer_q": nrm(ks[21], (D_MODEL, PEER_HEADS * PEER_DKEY), D_MODEL ** -0.5),
        "peer_sub_keys": nrm(ks[22], (PEER_HEADS, 2, PEER_KEYS, PEER_DKEY // 2), (PEER_DKEY // 2) ** -0.5),
        "peer_u": nrm(ks[23], (PEER_EXPERTS, D_MODEL), D_MODEL ** -0.5),
        "peer_v": nrm(ks[24], (PEER_EXPERTS, D_MODEL), 0.3),
        "g_final": gain(ks[25]),
    }


def reference(x_prompt, x_sample, cache_mem_k, cache_mem_v, cache_moba_k, cache_moba_v, cache_sb_k, cache_sb_v,
              page_table, mem_prompt, g_mix, w_in, w_br_moba, w_br_sb, w_out, g_mem_q, g_mem_kv, w_mem_q,
              w_mem_kv, w_mem_o, g_ffn, w_peer_q, peer_sub_keys, peer_u, peer_v, g_final):
    slopes = alibi_slopes()
    xp, xs = x_prompt, x_sample
    for _ in range(DEPTH):
        hp = rmsnorm(xp, g_mix)
        qa, moba_k_p, moba_v_p, qb, sb_k_p, sb_v_p, ga, gb = in_proj(hp, w_in)
        oa = moba_prompt(qa, moba_k_p, moba_v_p, slopes)
        ob = sb_prompt(qb, sb_k_p, sb_v_p)
        xp = xp + merge_branches(oa, ob, ga, gb, w_br_moba, w_br_sb, w_out)
        mem_k_p, mem_v_p = mem_kv(mem_prompt, g_mem_kv, w_mem_kv)
        xp = xp + mem_attend(rmsnorm(xp, g_mem_q), w_mem_q, mem_k_p, mem_v_p, w_mem_o)
        xp = xp + peer_ffn(rmsnorm(xp, g_ffn), w_peer_q, peer_sub_keys, peer_u, peer_v)
        hs = rmsnorm(xs, g_mix)
        qa_s, moba_k_s, moba_v_s, qb_s, sb_k_s, sb_v_s, ga_s, gb_s = in_proj(hs, w_in)
        oa_s = moba_sample(qa_s, moba_k_s, moba_v_s, cache_moba_k, cache_moba_v, page_table, slopes)
        ob_s = sb_sample(qb_s, sb_k_s, sb_v_s, cache_sb_k, cache_sb_v, page_table)
        xs = xs + merge_branches(oa_s, ob_s, ga_s, gb_s, w_br_moba, w_br_sb, w_out)
        xs = xs + mem_attend(rmsnorm(xs, g_mem_q), w_mem_q, cache_mem_k, cache_mem_v, w_mem_o)
        xs = xs + peer_ffn(rmsnorm(xs, g_ffn), w_peer_q, peer_sub_keys, peer_u, peer_v)
    y_prompt = rmsnorm(xp, g_final)
    y_sample = rmsnorm(xs, g_final)
    return (y_prompt, y_sample, mem_k_p, mem_v_p, moba_k_p, moba_v_p, sb_k_p, sb_v_p,
            moba_k_s, moba_v_s, sb_k_s, sb_v_s)
```

```python
import functools
import math

import jax
import jax.numpy as jnp
from jax import lax
from jax.experimental import pallas as pl
from jax.experimental.pallas import tpu as pltpu

F32 = jnp.float32
BF16 = jnp.bfloat16

RMS_EPS = 1e-6
MOBA_BLOCK = 256
MOBA_TOPK = 3
Q_BLOCK = 128
PAGE_SIZE = 128
N_HEADS = 8
HEAD_DIM = 64
W_ATT = N_HEADS * HEAD_DIM
N_HEADS_MEM = 4
HEAD_DIM_MEM = 128
PEER_HEADS = 8
PEER_KEYS = 128
PEER_TOPK = 16
NEG = -1e30
VMEM_LIMIT = 56 * 1024 * 1024


def _dot(a, b):
    return jnp.dot(a, b, preferred_element_type=F32)


def _dot_nt(a, b):
    return lax.dot_general(a, b, (((1,), (1,)), ((), ())), preferred_element_type=F32)


def _split3(x):
    hi = x.astype(BF16)
    r = x - hi.astype(F32)
    mid = r.astype(BF16)
    lo = (r - mid.astype(F32)).astype(BF16)
    return hi, mid, lo


def _dot_f32_by_exact(x, m):
    hi, mid, lo = _split3(x)
    return _dot(hi, m) + (_dot(mid, m) + _dot(lo, m))


def _dot_nt_precise(a, b):
    a0, a1, a2 = _split3(a)
    b0, b1, b2 = _split3(b)
    small = _dot_nt(a0, b2) + _dot_nt(a2, b0) + _dot_nt(a1, b1)
    return _dot_nt(a0, b0) + ((_dot_nt(a0, b1) + _dot_nt(a1, b0)) + small)


def _log_sigmoid(z):
    return jnp.minimum(z, 0.0) - jnp.log1p(jnp.exp(-jnp.abs(z)))


def _rmsnorm(x, g):
    ms = jnp.mean(x * x, axis=-1, keepdims=True)
    return x * lax.rsqrt(ms + RMS_EPS) * g


def _params(*sem):
    return pltpu.CompilerParams(dimension_semantics=sem, vmem_limit_bytes=VMEM_LIMIT)


def _norm_matmul_kernel(x_ref, g_ref, w_ref, *out_refs, splits, kmean_split):
    h = _rmsnorm(x_ref[...], g_ref[...]).astype(BF16)
    off = 0
    for i, width in enumerate(splits):
        y = _dot(h, w_ref[:, off:off + width])
        out_refs[i][...] = y
        if i == kmean_split:
            out_refs[len(splits)][0] = jnp.mean(y, axis=0, keepdims=True)
        off += width


def norm_matmul(x, g, w_bf16, splits, kmean_split=None, tm=256):
    n, d = x.shape
    assert n % tm == 0 and sum(splits) == w_bf16.shape[1]
    out_shape = [jax.ShapeDtypeStruct((n, s), F32) for s in splits]
    out_specs = [pl.BlockSpec((tm, s), lambda i: (i, 0)) for s in splits]
    if kmean_split is not None:
        assert tm == MOBA_BLOCK
        out_shape.append(jax.ShapeDtypeStruct((n // tm, 1, splits[kmean_split]), F32))
        out_specs.append(pl.BlockSpec((1, 1, splits[kmean_split]), lambda i: (i, 0, 0)))
    return pl.pallas_call(
        functools.partial(_norm_matmul_kernel, splits=tuple(splits), kmean_split=kmean_split),
        out_shape=out_shape,
        grid=(n // tm,),
        in_specs=[pl.BlockSpec((tm, d), lambda i: (i, 0)),
                  pl.BlockSpec((1, d), lambda i: (0, 0)),
                  pl.BlockSpec(w_bf16.shape, lambda i: (0, 0))],
        out_specs=out_specs,
        compiler_params=_params("parallel"),
        name="norm_matmul",
    )(x, g.reshape(1, d), w_bf16)


def _topk_mask(sc, valid, k):
    nbk = sc.shape[1]
    col = lax.broadcasted_iota(jnp.int32, sc.shape, 1).astype(F32)
    work = jnp.where(valid, sc, -jnp.inf)
    sel = jnp.zeros(sc.shape, F32)
    for _ in range(min(k, nbk)):
        m = jnp.max(work, axis=1, keepdims=True)
        idx = jnp.min(jnp.where(work == m, col, float(nbk)), axis=1, keepdims=True)
        hit = col == idx
        sel = jnp.where(hit, 1.0, sel)
        work = jnp.where(hit, -jnp.inf, work)
    return jnp.where(valid, sel, 0.0)


def _moba_prompt_kernel(slope_ref, q_ref, k_ref, v_ref, km_ref, o_ref):
    h = pl.program_id(1)
    qi = pl.program_id(2)
    slope = slope_ref[h]
    scale = HEAD_DIM ** -0.5
    q32 = q_ref[0, 0]
    qb = q32.astype(BF16)
    km = km_ref[0, 0]
    nb = km.shape[0]
    t0 = qi * Q_BLOCK
    own_b = t0 // MOBA_BLOCK

    sc = _dot_nt_precise(q32, km)
    colb = lax.broadcasted_iota(jnp.int32, (Q_BLOCK, nb), 1)
    sel = _topk_mask(sc, colb < own_b, MOBA_TOPK)

    t = t0 + lax.broadcasted_iota(jnp.int32, (Q_BLOCK, 1), 0)

    def logits(j):
        start = pl.multiple_of(j * MOBA_BLOCK, MOBA_BLOCK)
        kb = k_ref[0, 0, pl.ds(start, MOBA_BLOCK), :]
        vb = v_ref[0, 0, pl.ds(start, MOBA_BLOCK), :]
        kpos = start + lax.broadcasted_iota(jnp.int32, (1, MOBA_BLOCK), 1)
        dist = (t - kpos).astype(F32)
        return _dot_nt(qb, kb) * scale - slope * dist, kpos, vb

    s, kpos, vb = logits(own_b)
    s = jnp.where(kpos <= t, s, NEG)
    m = jnp.max(s, axis=1, keepdims=True)
    p = jnp.exp(s - m)
    l = jnp.sum(p, axis=1, keepdims=True)
    acc = _dot(p.astype(BF16), vb)

    def body(j, carry):
        m, l, acc = carry
        s, _, vb = logits(j)
        picked = jnp.sum(jnp.where(colb == j, sel, 0.0), axis=1, keepdims=True)
        s = jnp.where(picked > 0.0, s, NEG)
        m_new = jnp.maximum(m, jnp.max(s, axis=1, keepdims=True))
        alpha = jnp.exp(m - m_new)
        p = jnp.exp(s - m_new)
        l = alpha * l + jnp.sum(p, axis=1, keepdims=True)
        acc = alpha * acc + _dot(p.astype(BF16), vb)
        return m_new, l, acc

    m, l, acc = lax.fori_loop(0, own_b, body, (m, l, acc))
    o_ref[0, 0] = acc / l


def moba_prompt(q_hm, k_hm, v_hm, kmean_hm, slopes):
    B, H, T, Dh = q_hm.shape
    nb = kmean_hm.shape[2]
    assert T % MOBA_BLOCK == 0 and Dh == HEAD_DIM
    return pl.pallas_call(
        _moba_prompt_kernel,
        out_shape=jax.ShapeDtypeStruct((B, H, T, Dh), F32),
        grid=(B, H, T // Q_BLOCK),
        in_specs=[pl.BlockSpec(memory_space=pltpu.SMEM),
                  pl.BlockSpec((1, 1, Q_BLOCK, Dh), lambda b, h, i: (b, h, i, 0)),
                  pl.BlockSpec((1, 1, T, Dh), lambda b, h, i: (b, h, 0, 0)),
                  pl.BlockSpec((1, 1, T, Dh), lambda b, h, i: (b, h, 0, 0)),
                  pl.BlockSpec((1, 1, nb, Dh), lambda b, h, i: (b, h, 0, 0))],
        out_specs=pl.BlockSpec((1, 1, Q_BLOCK, Dh), lambda b, h, i: (b, h, i, 0)),
        compiler_params=_params("parallel", "parallel", "arbitrary"),
        name="moba_prompt",
    )(slopes, q_hm, k_hm, v_hm, kmean_hm)


def _tri_later(n):
    r = lax.broadcasted_iota(jnp.int32, (n, n), 0)
    c = lax.broadcasted_iota(jnp.int32, (n, n), 1)
    return (r > c).astype(BF16)


def _sb_chunk(qb, kc, vc, tri, log_surv, acc, valid, scale):
    z = _dot_nt(qb, kc) * scale
    ls = _log_sigmoid(z)
    lf = ls - z
    if valid is not None:
        lf = jnp.where(valid, lf, 0.0)
    after = _dot_f32_by_exact(lf, tri)
    a = jnp.exp(ls + after + log_surv)
    if valid is not None:
        a = jnp.where(valid, a, 0.0)
    acc = acc + _dot(a.astype(BF16), vc)
    log_surv = log_surv + jnp.sum(lf, axis=1, keepdims=True)
    return log_surv, acc


def _sb_prompt_kernel(q_ref, k_ref, v_ref, o_ref):
    qi = pl.program_id(2)
    scale = HEAD_DIM ** -0.5
    qb = q_ref[0, 0]
    tri = _tri_later(Q_BLOCK)

    def chunk(c):
        start = pl.multiple_of(c * Q_BLOCK, Q_BLOCK)
        return k_ref[0, 0, pl.ds(start, Q_BLOCK), :], v_ref[0, 0, pl.ds(start, Q_BLOCK), :]

    r = lax.broadcasted_iota(jnp.int32, (Q_BLOCK, Q_BLOCK), 0)
    c = lax.broadcasted_iota(jnp.int32, (Q_BLOCK, Q_BLOCK), 1)
    kc, vc = chunk(qi)
    log_surv, acc = _sb_chunk(qb, kc, vc, tri, jnp.zeros((Q_BLOCK, 1), F32),
                              jnp.zeros((Q_BLOCK, HEAD_DIM), F32), c < r, scale)

    def body(i, carry):
        log_surv, acc = carry
        kc, vc = chunk(qi - 1 - i)
        return _sb_chunk(qb, kc, vc, tri, log_surv, acc, None, scale)

    log_surv, acc = lax.fori_loop(0, qi, body, (log_surv, acc))
    o_ref[0, 0] = acc


def sb_prompt(q_hm, k_hm, v_hm):
    B, H, T, Dh = q_hm.shape
    return pl.pallas_call(
        _sb_prompt_kernel,
        out_shape=jax.ShapeDtypeStruct((B, H, T, Dh), F32),
        grid=(B, H, T // Q_BLOCK),
        in_specs=[pl.BlockSpec((1, 1, Q_BLOCK, Dh), lambda b, h, i: (b, h, i, 0)),
                  pl.BlockSpec((1, 1, T, Dh), lambda b, h, i: (b, h, 0, 0)),
                  pl.BlockSpec((1, 1, T, Dh), lambda b, h, i: (b, h, 0, 0))],
        out_specs=pl.BlockSpec((1, 1, Q_BLOCK, Dh), lambda b, h, i: (b, h, i, 0)),
        compiler_params=_params("parallel", "parallel", "arbitrary"),
        name="sb_prompt",
    )(q_hm, k_hm, v_hm)


PAGES_PER_STEP = 8
NEW_ROWS = 8


def _sample_rows(qrep_ref):
    q = qrep_ref[0]
    rows, w = q.shape
    r = lax.broadcasted_iota(jnp.int32, (rows, w), 0)
    c = lax.broadcasted_iota(jnp.int32, (rows, w), 1)
    head_mask = (c // HEAD_DIM) == (r % N_HEADS)
    return jnp.where(head_mask, q, 0.0), head_mask


def _new_page(new_ref):
    x = new_ref[0]
    return jnp.concatenate([x, jnp.zeros((PAGE_SIZE - x.shape[0], x.shape[1]), x.dtype)], axis=0)


def _write_rows_per_query(o_ref, acc, head_mask, n_q):
    accm = jnp.where(head_mask, acc, 0.0)
    for i in range(n_q):
        o_ref[0, i:i + 1, :] = jnp.sum(accm[i * N_HEADS:(i + 1) * N_HEADS], axis=0, keepdims=True)


def _moba_sample_kernel(pt_ref, qrep_ref, knew_ref, vnew_ref, slope_ref, *refs, n_pages, n_q):
    pg = PAGES_PER_STEP
    k_refs, v_refs = refs[:pg], refs[pg:2 * pg]
    o_ref, logit_ref, bsum_ref, acc_ref, l_ref = refs[2 * pg:]
    g = pl.program_id(1)
    ng = n_pages // pg
    ppb = MOBA_BLOCK // PAGE_SIZE
    nblk = n_pages // ppb
    rows = n_q * N_HEADS
    scale = HEAD_DIM ** -0.5
    qm, head_mask = _sample_rows(qrep_ref)
    qb = qm.astype(BF16)

    @pl.when(g < ng)
    def _keys():
        for i in range(pg):
            page = g * pg + i
            kp = k_refs[i][0]
            logit_ref[page] = _dot_nt(qb, kp.astype(BF16))
            bsum_ref[page] = jnp.sum(kp, axis=0, keepdims=True)

    @pl.when(g == ng - 1)
    def _select_and_softmax():
        psum = bsum_ref[...]
        kmean = jnp.concatenate(
            [sum(psum[b * ppb + j] for j in range(ppb)) for b in range(nblk)], axis=0) * (1.0 / MOBA_BLOCK)
        sc = _dot_nt_precise(qm, kmean)
        sel = _topk_mask(sc, lax.broadcasted_iota(jnp.int32, sc.shape, 1) >= 0, MOBA_TOPK)
        slope = slope_ref[...]
        past = n_pages * PAGE_SIZE
        qidx = lax.broadcasted_iota(jnp.int32, (rows, PAGE_SIZE), 0) // N_HEADS
        col = lax.broadcasted_iota(jnp.int32, (rows, PAGE_SIZE), 1)
        s_new = _dot_nt(qb, _new_page(knew_ref).astype(BF16)) * scale - slope * (qidx - col).astype(F32)
        s_new = jnp.where((col <= qidx) & (col < n_q), s_new, NEG)
        m = jnp.max(s_new, axis=1, keepdims=True)
        for p in range(n_pages):
            dist = (past + qidx - (p * PAGE_SIZE + col)).astype(F32)
            s = logit_ref[p] * scale - slope * dist
            s = jnp.where(sel[:, p // ppb:p // ppb + 1] > 0.0, s, NEG)
            logit_ref[p] = s
            m = jnp.maximum(m, jnp.max(s, axis=1, keepdims=True))
        p_new = jnp.exp(s_new - m)
        l = jnp.sum(p_new, axis=1, keepdims=True)
        for p in range(n_pages):
            e = jnp.exp(logit_ref[p] - m)
            logit_ref[p] = e
            l = l + jnp.sum(e, axis=1, keepdims=True)
        l_ref[...] = l
        acc_ref[...] = _dot(p_new.astype(BF16), _new_page(vnew_ref).astype(BF16))

    @pl.when(g >= ng)
    def _values():
        acc = acc_ref[...]
        for i in range(pg):
            page = (g - ng) * pg + i
            acc = acc + _dot(logit_ref[page].astype(BF16), v_refs[i][0].astype(BF16))
        acc_ref[...] = acc

    @pl.when(g == 2 * ng - 1)
    def _finish():
        _write_rows_per_query(o_ref, acc_ref[...] / l_ref[...], head_mask, n_q)


def _page_specs(n_pages, page_of_step, width):
    def spec(i):
        return pl.BlockSpec((1, PAGE_SIZE, width),
                            lambda b, g, pt: (pt[b * n_pages + page_of_step(g, i)], 0, 0))
    return [spec(i) for i in range(PAGES_PER_STEP)]


def _sample_prep(q, k_new, v_new):
    db, n_q, w = q.shape
    qrep = jnp.repeat(q, N_HEADS, axis=1)
    pad = ((0, 0), (0, NEW_ROWS - n_q), (0, 0))
    return qrep, jnp.pad(k_new, pad), jnp.pad(v_new, pad)


def moba_sample(q, k_new, v_new, pool_k, pool_v, page_table, slopes):
    db, n_q, w = q.shape
    n_pages = page_table.shape[1]
    pg = PAGES_PER_STEP
    assert n_pages % pg == 0 and (n_pages * PAGE_SIZE) % MOBA_BLOCK == 0 and n_q <= NEW_ROWS
    ng = n_pages // pg
    rows = n_q * N_HEADS
    qrep, kn, vn = _sample_prep(q, k_new, v_new)
    slope_rows = jnp.tile(slopes, n_q).reshape(rows, 1)
    per_sample = lambda shape: pl.BlockSpec((1,) + shape, lambda b, g, pt: (b, 0, 0))
    grid_spec = pltpu.PrefetchScalarGridSpec(
        num_scalar_prefetch=1,
        grid=(db, 2 * ng),
        in_specs=[per_sample((rows, w)), per_sample((NEW_ROWS, w)), per_sample((NEW_ROWS, w)),
                  pl.BlockSpec((rows, 1), lambda b, g, pt: (0, 0))]
                 + _page_specs(n_pages, lambda g, i: jnp.minimum(g, ng - 1) * pg + i, w)
                 + _page_specs(n_pages, lambda g, i: jnp.maximum(g - ng, 0) * pg + i, w),
        out_specs=per_sample((n_q, w)),
        scratch_shapes=[pltpu.VMEM((n_pages, rows, PAGE_SIZE), F32),
                        pltpu.VMEM((n_pages, 1, w), F32),
                        pltpu.VMEM((rows, w), F32),
                        pltpu.VMEM((rows, 1), F32)],
    )
    return pl.pallas_call(
        functools.partial(_moba_sample_kernel, n_pages=n_pages, n_q=n_q),
        out_shape=jax.ShapeDtypeStruct((db, n_q, w), F32),
        grid_spec=grid_spec,
        compiler_params=_params("parallel", "arbitrary"),
        name="moba_sample",
    )(page_table.reshape(-1), qrep, kn, vn, slope_rows, *([pool_k] * pg), *([pool_v] * pg))


def _sb_sample_kernel(pt_ref, qrep_ref, knew_ref, vnew_ref, *refs, n_q):
    pg = PAGES_PER_STEP
    k_refs, v_refs = refs[:pg], refs[pg:2 * pg]
    o_ref, acc_ref, surv_ref = refs[2 * pg:]
    g = pl.program_id(1)
    rows = n_q * N_HEADS
    scale = HEAD_DIM ** -0.5
    qm, head_mask = _sample_rows(qrep_ref)
    qb = qm.astype(BF16)
    tri = _tri_later(PAGE_SIZE)

    @pl.when(g == 0)
    def _new_tokens():
        qidx = lax.broadcasted_iota(jnp.int32, (rows, PAGE_SIZE), 0) // N_HEADS
        col = lax.broadcasted_iota(jnp.int32, (rows, PAGE_SIZE), 1)
        valid = (col < qidx) & (col < n_q)
        surv, acc = _sb_chunk(qb, _new_page(knew_ref).astype(BF16), _new_page(vnew_ref).astype(BF16), tri,
                              jnp.zeros((rows, 1), F32), jnp.zeros(acc_ref.shape, F32), valid, scale)
        surv_ref[...] = surv
        acc_ref[...] = acc

    surv, acc = surv_ref[...], acc_ref[...]
    for i in range(pg):
        surv, acc = _sb_chunk(qb, k_refs[i][0].astype(BF16), v_refs[i][0].astype(BF16), tri, surv, acc, None, scale)
    surv_ref[...] = surv
    acc_ref[...] = acc

    @pl.when(g == pl.num_programs(1) - 1)
    def _finish():
        _write_rows_per_query(o_ref, acc, head_mask, n_q)


def sb_sample(q, k_new, v_new, pool_k, pool_v, page_table):
    db, n_q, w = q.shape
    n_pages = page_table.shape[1]
    pg = PAGES_PER_STEP
    assert n_pages % pg == 0 and n_q <= NEW_ROWS
    rows = n_q * N_HEADS
    qrep, kn, vn = _sample_prep(q, k_new, v_new)
    per_sample = lambda shape: pl.BlockSpec((1,) + shape, lambda b, g, pt: (b, 0, 0))
    newest_first = lambda g, i: n_pages - 1 - (g * pg + i)
    grid_spec = pltpu.PrefetchScalarGridSpec(
        num_scalar_prefetch=1,
        grid=(db, n_pages // pg),
        in_specs=[per_sample((rows, w)), per_sample((NEW_ROWS, w)), per_sample((NEW_ROWS, w))]
                 + _page_specs(n_pages, newest_first, w) + _page_specs(n_pages, newest_first, w),
        out_specs=per_sample((n_q, w)),
        scratch_shapes=[pltpu.VMEM((rows, w), F32), pltpu.VMEM((rows, 1), F32)],
    )
    return pl.pallas_call(
        functools.partial(_sb_sample_kernel, n_q=n_q),
        out_shape=jax.ShapeDtypeStruct((db, n_q, w), F32),
        grid_spec=grid_spec,
        compiler_params=_params("parallel", "arbitrary"),
        name="sb_sample",
    )(page_table.reshape(-1), qrep, kn, vn, *([pool_k] * pg), *([pool_v] * pg))


def _merge_mem_kernel(x_ref, oa_ref, ob_ref, ga_ref, gb_ref, mk_ref, mv_ref, wa_ref, wb_ref, wo_ref,
                      wmq_ref, wmo_ref, gq_ref, gf_ref, x2_ref, hb_ref):
    tm = x_ref.shape[0]
    n_mem, m_len, w_mem = mk_ref.shape
    ya = _dot(oa_ref[...], wa_ref[...])
    yb = _dot(ob_ref[...], wb_ref[...])
    mix = jax.nn.sigmoid(ga_ref[...]) * ya + jax.nn.sigmoid(gb_ref[...]) * yb
    x1 = x_ref[...] + _dot(mix.astype(BF16), wo_ref[...])

    qm = _dot(_rmsnorm(x1, gq_ref[...]).astype(BF16), wmq_ref[...]).astype(BF16)
    mk = mk_ref[...].reshape(n_mem * m_len, w_mem).astype(BF16)
    mv = mv_ref[...].reshape(n_mem * m_len, w_mem).astype(BF16)
    if n_mem > 1:
        row_mem = lax.broadcasted_iota(jnp.int32, (tm, n_mem * m_len), 0) // (tm // n_mem)
        key_mem = lax.broadcasted_iota(jnp.int32, (tm, n_mem * m_len), 1) // m_len
        same = row_mem == key_mem
    heads = []
    for hd in range(N_HEADS_MEM):
        sl = slice(hd * HEAD_DIM_MEM, (hd + 1) * HEAD_DIM_MEM)
        lg = _dot_nt(qm[:, sl], mk[:, sl]) * (HEAD_DIM_MEM ** -0.5)
        if n_mem > 1:
            lg = jnp.where(same, lg, NEG)
        p = jnp.exp(lg - jnp.max(lg, axis=1, keepdims=True))
        heads.append(_dot(p.astype(BF16), mv[:, sl]) / jnp.sum(p, axis=1, keepdims=True))
    o = jnp.concatenate(heads, axis=1)
    x2 = x1 + _dot(o.astype(BF16), wmo_ref[...])
    x2_ref[...] = x2
    hb_ref[...] = _rmsnorm(x2, gf_ref[...]).astype(BF16)


def merge_mem(x, oa, ob, ga, gb, mem_k, mem_v, wa, wb, wo, wmq, wmo, g_mem_q, g_ffn, tm):
    n, d = x.shape
    n_mem_total, m_len, w_mem = mem_k.shape
    rows_per_mem = n // n_mem_total
    assert n % tm == 0 and (tm % rows_per_mem == 0 or rows_per_mem % tm == 0)
    if rows_per_mem >= tm:
        gm = 1
        mem_map = lambda i: (i * tm // rows_per_mem, 0, 0)
    else:
        gm = tm // rows_per_mem
        mem_map = lambda i: (i, 0, 0)
    row = lambda width: pl.BlockSpec((tm, width), lambda i: (i, 0))
    full = lambda a: pl.BlockSpec(a.shape, lambda i: (0,) * a.ndim)
    gq, gf = g_mem_q.reshape(1, d), g_ffn.reshape(1, d)
    return pl.pallas_call(
        _merge_mem_kernel,
        out_shape=[jax.ShapeDtypeStruct((n, d), F32), jax.ShapeDtypeStruct((n, d), BF16)],
        grid=(n // tm,),
        in_specs=[row(d), row(oa.shape[1]), row(ob.shape[1]), row(d), row(d),
                  pl.BlockSpec((gm, m_len, w_mem), mem_map), pl.BlockSpec((gm, m_len, w_mem), mem_map),
                  full(wa), full(wb), full(wo), full(wmq), full(wmo), full(gq), full(gf)],
        out_specs=[row(d), row(d)],
        compiler_params=_params("parallel"),
        name="merge_mem",
    )(x, oa, ob, ga, gb, mem_k, mem_v, wa, wb, wo, wmq, wmo, gq, gf)


def _peer_query_kernel(hb_ref, wqt_ref, subk_ref, hbt_ref, s_ref):
    hb = hb_ref[...]
    hbt_ref[...] = jnp.transpose(hb.astype(F32)).astype(BF16)
    qt = _dot_nt(wqt_ref[...], hb)
    for hc in range(subk_ref.shape[0]):
        q_hc = qt[hc * PEER_KEYS:(hc + 1) * PEER_KEYS].astype(BF16)
        s_ref[hc] = _dot(subk_ref[hc], q_hc)


def peer_query(hb, wq_t, subk, tn=512):
    n, d = hb.shape
    assert n % tn == 0 and subk.shape[2] == PEER_KEYS
    return pl.pallas_call(
        _peer_query_kernel,
        out_shape=[jax.ShapeDtypeStruct((d, n), BF16), jax.ShapeDtypeStruct((subk.shape[0], PEER_KEYS, n), F32)],
        grid=(n // tn,),
        in_specs=[pl.BlockSpec((tn, d), lambda i: (i, 0)),
                  pl.BlockSpec(wq_t.shape, lambda i: (0, 0)),
                  pl.BlockSpec(subk.shape, lambda i: (0, 0, 0))],
        out_specs=[pl.BlockSpec((d, tn), lambda i: (0, i)),
                   pl.BlockSpec((subk.shape[0], PEER_KEYS, tn), lambda i: (0, 0, i))],
        compiler_params=_params("parallel"),
        name="peer_query",
    )(hb, wq_t, subk)


def _top_ranks(s, k):
    n_keys = s.shape[0]
    kio = lax.broadcasted_iota(jnp.int32, s.shape, 0).astype(F32)
    work = s
    rank = jnp.full(s.shape, float(k), F32)
    vals = []
    for r in range(k):
        m = jnp.max(work, axis=0, keepdims=True)
        idx = jnp.min(jnp.where(work == m, kio, float(n_keys)), axis=0, keepdims=True)
        hit = kio == idx
        rank = jnp.where(hit, float(r), rank)
        work = jnp.where(hit, -jnp.inf, work)
        vals.append(m)
    return rank, vals


def _route_kernel(s_ref, cnt_ref, e1_ref, rank2_ref, e2_ref, cand_ref):
    k = PEER_TOPK
    s1, s2 = s_ref[0], s_ref[1]
    tn = s1.shape[1]
    rank1, v1 = _top_ranks(s1, k)
    rank2, v2 = _top_ranks(s2, k)
    rio = lax.broadcasted_iota(jnp.int32, (k, tn), 0)
    v2_all = jnp.zeros((k, tn), F32)
    for r in range(k):
        v2_all = jnp.where(rio == r, v2[r], v2_all)
    for r in range(k):
        cand_ref[r * k:(r + 1) * k, :] = v1[r] + v2_all
    cand = cand_ref[...]
    chosen, _ = _top_ranks(cand, k)
    chosen = chosen < float(k)
    top0 = v1[0] + v2[0]
    z = jnp.sum(jnp.where(chosen, jnp.exp(cand - top0), 0.0), axis=0, keepdims=True)
    cnt = jnp.zeros(s1.shape, F32)
    for r in range(k):
        c_r = jnp.sum(jnp.where(chosen[r * k:(r + 1) * k], 1.0, 0.0), axis=0, keepdims=True)
        cnt = jnp.where(rank1 == float(r), c_r, cnt)
    cnt_ref[0] = cnt
    e1_ref[0] = jnp.exp(s1 - v1[0]) / z
    rank2_ref[0] = rank2
    e2_ref[0] = jnp.exp(s2 - v2[0])


def peer_route(scores, tn=256):
    hc, n_keys, n = scores.shape
    assert n % tn == 0
    out = jax.ShapeDtypeStruct((hc // 2, n_keys, n), F32)
    spec = pl.BlockSpec((1, n_keys, tn), lambda i, h: (h, 0, i))
    return pl.pallas_call(
        _route_kernel,
        out_shape=[out] * 4,
        grid=(n // tn, hc // 2),
        in_specs=[pl.BlockSpec((2, n_keys, tn), lambda i, h: (h, 0, i))],
        out_specs=[spec] * 4,
        scratch_shapes=[pltpu.VMEM((PEER_TOPK * PEER_TOPK, tn), F32)],
        compiler_params=_params("parallel", "parallel"),
        name="peer_route",
    )(scores)


EXPERT_TILE = 512


def _gelu_exact(x):
    return 0.5 * x * (1.0 + lax.erf(x * (2.0 ** -0.5)))


def _peer_dense_kernel(hbt_ref, u_ref, vt_ref, cnt_ref, e1_ref, rank2_ref, e2_ref, x2_ref, gfin_ref,
                       y_ref, acc_ref, wt_ref):
    e = pl.program_id(1)

    @pl.when(e == 0)
    def _init():
        acc_ref[...] = jnp.zeros(acc_ref.shape, F32)

    act = _dot(u_ref[...], hbt_ref[...])
    for ai in range(EXPERT_TILE // PEER_KEYS):
        a = e * (EXPERT_TILE // PEER_KEYS) + ai
        gate = jnp.zeros((PEER_KEYS, act.shape[1]), F32)
        for h in range(PEER_HEADS):
            cnt = cnt_ref[h, pl.ds(a, 1), :]
            e1 = e1_ref[h, pl.ds(a, 1), :]
            gate = gate + jnp.where(rank2_ref[h] < cnt, e2_ref[h], 0.0) * e1
        rows = slice(ai * PEER_KEYS, (ai + 1) * PEER_KEYS)
        wt_ref[rows, :] = (gate * _gelu_exact(act[rows])).astype(BF16)
    acc_ref[...] += _dot(vt_ref[...], wt_ref[...])

    @pl.when(e == pl.num_programs(1) - 1)
    def _finish():
        y_ref[...] = _rmsnorm(x2_ref[...] + jnp.transpose(acc_ref[...]), gfin_ref[...])


def peer_dense(hbt, u_bf16, vt_bf16, cnt, e1, rank2, e2, x2, g_final, tn=512):
    d, n = hbt.shape
    n_exp = u_bf16.shape[0]
    assert n % tn == 0 and n_exp % EXPERT_TILE == 0 and n_exp == PEER_KEYS * PEER_KEYS
    route_spec = pl.BlockSpec((PEER_HEADS, PEER_KEYS, tn), lambda i, e: (0, 0, i))
    return pl.pallas_call(
        _peer_dense_kernel,
        out_shape=jax.ShapeDtypeStruct((n, d), F32),
        grid=(n // tn, n_exp // EXPERT_TILE),
        in_specs=[pl.BlockSpec((d, tn), lambda i, e: (0, i)),
                  pl.BlockSpec((EXPERT_TILE, d), lambda i, e: (e, 0)),
                  pl.BlockSpec((d, EXPERT_TILE), lambda i, e: (0, e)),
                  route_spec, route_spec, route_spec, route_spec,
                  pl.BlockSpec((tn, d), lambda i, e: (i, 0)),
                  pl.BlockSpec((1, d), lambda i, e: (0, 0))],
        out_specs=pl.BlockSpec((tn, d), lambda i, e: (i, 0)),
        scratch_shapes=[pltpu.VMEM((d, tn), F32), pltpu.VMEM((EXPERT_TILE, tn), BF16)],
        compiler_params=_params("parallel", "arbitrary"),
        name="peer_dense",
    )(hbt, u_bf16, vt_bf16, cnt, e1, rank2, e2, x2, g_final.reshape(1, d))


def peer_ffn_residual_norm(x2, hb, wq_t, subk, u_bf16, vt_bf16, g_final):
    hbt, scores = peer_query(hb, wq_t, subk)
    cnt, e1, rank2, e2 = peer_route(scores)
    return peer_dense(hbt, u_bf16, vt_bf16, cnt, e1, rank2, e2, x2, g_final)


def kernel(x_prompt, x_sample, cache_mem_k, cache_mem_v, cache_moba_k, cache_moba_v, cache_sb_k, cache_sb_v, page_table, mem_prompt, g_mix, w_in, w_br_moba, w_br_sb, w_out, g_mem_q, g_mem_kv, w_mem_q, w_mem_kv, w_mem_o, g_ffn, w_peer_q, peer_sub_keys, peer_u, peer_v, g_final):
    B, T, D = x_prompt.shape
    DB, QN, _ = x_sample.shape
    n_mem = mem_prompt.shape[1]
    w_mem = N_HEADS_MEM * HEAD_DIM_MEM
    slopes = 2.0 ** (-(8.0 / N_HEADS) * jnp.arange(1, N_HEADS + 1, dtype=F32))
    splits = (W_ATT,) * 6 + (D, D)

    w_in_b = w_in.astype(BF16)
    wa, wb, wo = w_br_moba.astype(BF16), w_br_sb.astype(BF16), w_out.astype(BF16)
    wmq, wmo = w_mem_q.astype(BF16), w_mem_o.astype(BF16)
    wq_t = w_peer_q.T.astype(BF16)
    subk = peer_sub_keys.reshape(2 * PEER_HEADS, PEER_KEYS, -1).astype(BF16)
    u_b = peer_u.astype(BF16)
    vt_b = peer_v.T.astype(BF16)

    qa, ka, va, qb, kb, vb, ga, gb, kmean = norm_matmul(x_prompt.reshape(B * T, D), g_mix, w_in_b, splits, kmean_split=1)
    hm = lambda a: a.reshape(B, T, N_HEADS, HEAD_DIM).transpose(0, 2, 1, 3)
    km = kmean.reshape(B, T // MOBA_BLOCK, N_HEADS, HEAD_DIM).transpose(0, 2, 1, 3)
    oa = moba_prompt(hm(qa), hm(ka).astype(BF16), hm(va).astype(BF16), km, slopes)
    ob = sb_prompt(hm(qb).astype(BF16), hm(kb).astype(BF16), hm(vb).astype(BF16))
    tok = lambda o: o.transpose(0, 2, 1, 3).reshape(B * T, W_ATT).astype(BF16)
    mem_k_p, mem_v_p = norm_matmul(mem_prompt.reshape(B * n_mem, D), g_mem_kv, w_mem_kv.astype(BF16), (w_mem, w_mem))
    x2, hb = merge_mem(x_prompt.reshape(B * T, D), tok(oa), tok(ob), ga, gb,
                       mem_k_p.reshape(B, n_mem, w_mem), mem_v_p.reshape(B, n_mem, w_mem),
                       wa, wb, wo, wmq, wmo, g_mem_q, g_ffn, tm=256)
    y_prompt = peer_ffn_residual_norm(x2, hb, wq_t, subk, u_b, vt_b, g_final).reshape(B, T, D)

    qa_s, ka_s, va_s, qb_s, kb_s, vb_s, ga_s, gb_s = norm_matmul(x_sample.reshape(DB * QN, D), g_mix, w_in_b, splits)
    per_sample = lambda a: a.reshape(DB, QN, W_ATT)
    pool = lambda c: c.reshape(c.shape[0], PAGE_SIZE, W_ATT)
    oa_s = moba_sample(per_sample(qa_s), per_sample(ka_s), per_sample(va_s), pool(cache_moba_k), pool(cache_moba_v),
                       page_table, slopes)
    ob_s = sb_sample(per_sample(qb_s), per_sample(kb_s), per_sample(vb_s), pool(cache_sb_k), pool(cache_sb_v), page_table)
    x2_s, hb_s = merge_mem(x_sample.reshape(DB * QN, D), oa_s.reshape(DB * QN, W_ATT).astype(BF16),
                           ob_s.reshape(DB * QN, W_ATT).astype(BF16), ga_s, gb_s,
                           cache_mem_k.reshape(DB, n_mem, w_mem), cache_mem_v.reshape(DB, n_mem, w_mem),
                           wa, wb, wo, wmq, wmo, g_mem_q, g_ffn, tm=8 * QN)
    y_sample = peer_ffn_residual_norm(x2_s, hb_s, wq_t, subk, u_b, vt_b, g_final).reshape(DB, QN, D)

    heads = lambda a, lead: a.reshape(lead + (N_HEADS, HEAD_DIM))
    mem_heads = lambda a: a.reshape(B, n_mem, N_HEADS_MEM, HEAD_DIM_MEM)
    return (y_prompt, y_sample, mem_heads(mem_k_p), mem_heads(mem_v_p),
            heads(ka, (B, T)), heads(va, (B, T)), heads(kb, (B, T)), heads(vb, (B, T)),
            heads(ka_s, (DB, QN)), heads(va_s, (DB, QN)), heads(kb_s, (DB, QN)), heads(vb_s, (DB, QN)))
```

```python
import functools
import math

import jax
import jax.numpy as jnp
from jax import lax
from jax.experimental import pallas as pl
from jax.experimental.pallas import tpu as pltpu

F32 = jnp.float32
BF16 = jnp.bfloat16

RMS_EPS = 1e-6
MOBA_BLOCK = 256
MOBA_TOPK = 3
Q_BLOCK = 128
PAGE_SIZE = 128
N_HEADS = 8
HEAD_DIM = 64
W_ATT = N_HEADS * HEAD_DIM
N_HEADS_MEM = 4
HEAD_DIM_MEM = 128
PEER_HEADS = 8
PEER_KEYS = 128
PEER_TOPK = 16
NEG = -1e30
SB_DEAD = -105.0
SB_TILE = 256
VMEM_LIMIT = 56 * 1024 * 1024


def _dot(a, b):
    return jnp.dot(a, b, preferred_element_type=F32)


def _dot_nt(a, b):
    return lax.dot_general(a, b, (((1,), (1,)), ((), ())), preferred_element_type=F32)


def _split3(x):
    hi = x.astype(BF16)
    r = x - hi.astype(F32)
    mid = r.astype(BF16)
    lo = (r - mid.astype(F32)).astype(BF16)
    return hi, mid, lo


def _dot_f32_by_exact(x, m):
    hi, mid, lo = _split3(x)
    return _dot(hi, m) + (_dot(mid, m) + _dot(lo, m))


def _dot_nt_precise(a, b):
    a0, a1, a2 = _split3(a)
    b0, b1, b2 = _split3(b)
    small = _dot_nt(a0, b2) + _dot_nt(a2, b0) + _dot_nt(a1, b1)
    return _dot_nt(a0, b0) + ((_dot_nt(a0, b1) + _dot_nt(a1, b0)) + small)


def _log_sigmoid(z):
    return jnp.minimum(z, 0.0) - jnp.log1p(jnp.exp(-jnp.abs(z)))


def _rmsnorm(x, g):
    ms = jnp.mean(x * x, axis=-1, keepdims=True)
    return x * lax.rsqrt(ms + RMS_EPS) * g


def _params(*sem):
    return pltpu.CompilerParams(dimension_semantics=sem, vmem_limit_bytes=VMEM_LIMIT)


def _norm_matmul_kernel(x_ref, g_ref, w_ref, *out_refs, splits, kmean_split):
    h = _rmsnorm(x_ref[...], g_ref[...]).astype(BF16)
    off = 0
    for i, width in enumerate(splits):
        y = _dot(h, w_ref[:, off:off + width])
        out_refs[i][...] = y
        if i == kmean_split:
            out_refs[len(splits)][0] = jnp.mean(y, axis=0, keepdims=True)
        off += width


def norm_matmul(x, g, w_bf16, splits, kmean_split=None, tm=256):
    n, d = x.shape
    assert n % tm == 0 and sum(splits) == w_bf16.shape[1]
    out_shape = [jax.ShapeDtypeStruct((n, s), F32) for s in splits]
    out_specs = [pl.BlockSpec((tm, s), lambda i: (i, 0)) for s in splits]
    if kmean_split is not None:
        assert tm == MOBA_BLOCK
        out_shape.append(jax.ShapeDtypeStruct((n // tm, 1, splits[kmean_split]), F32))
        out_specs.append(pl.BlockSpec((1, 1, splits[kmean_split]), lambda i: (i, 0, 0)))
    return pl.pallas_call(
        functools.partial(_norm_matmul_kernel, splits=tuple(splits), kmean_split=kmean_split),
        out_shape=out_shape,
        grid=(n // tm,),
        in_specs=[pl.BlockSpec((tm, d), lambda i: (i, 0)),
                  pl.BlockSpec((1, d), lambda i: (0, 0)),
                  pl.BlockSpec(w_bf16.shape, lambda i: (0, 0))],
        out_specs=out_specs,
        compiler_params=_params("parallel"),
        name="norm_matmul",
    )(x, g.reshape(1, d), w_bf16)


def _topk_mask(sc, valid, k):
    nbk = sc.shape[1]
    col = lax.broadcasted_iota(jnp.int32, sc.shape, 1).astype(F32)
    work = jnp.where(valid, sc, -jnp.inf)
    sel = jnp.zeros(sc.shape, F32)
    for _ in range(min(k, nbk)):
        m = jnp.max(work, axis=1, keepdims=True)
        idx = jnp.min(jnp.where(work == m, col, float(nbk)), axis=1, keepdims=True)
        hit = col == idx
        sel = jnp.where(hit, 1.0, sel)
        work = jnp.where(hit, -jnp.inf, work)
    return jnp.where(valid, sel, 0.0)


def _moba_prompt_kernel(slope_ref, q_ref, k_ref, v_ref, km_ref, o_ref):
    h = pl.program_id(1)
    own_b = pl.program_id(2)
    slope = slope_ref[h]
    scale = HEAD_DIM ** -0.5
    q32 = q_ref[0, 0]
    qb = q32.astype(BF16)
    km = km_ref[0, 0]
    nb = km.shape[0]
    t0 = own_b * MOBA_BLOCK

    sc = _dot_nt_precise(q32, km)
    colb = lax.broadcasted_iota(jnp.int32, (MOBA_BLOCK, nb), 1)
    sel = _topk_mask(sc, colb < own_b, MOBA_TOPK)

    t = t0 + lax.broadcasted_iota(jnp.int32, (MOBA_BLOCK, 1), 0)

    def logits(j):
        start = pl.multiple_of(j * MOBA_BLOCK, MOBA_BLOCK)
        kb = k_ref[0, 0, pl.ds(start, MOBA_BLOCK), :]
        vb = v_ref[0, 0, pl.ds(start, MOBA_BLOCK), :]
        kpos = start + lax.broadcasted_iota(jnp.int32, (1, MOBA_BLOCK), 1)
        dist = (t - kpos).astype(F32)
        return _dot_nt(qb, kb) * scale - slope * dist, kpos, vb

    s, kpos, vb = logits(own_b)
    s = jnp.where(kpos <= t, s, NEG)
    m = jnp.max(s, axis=1, keepdims=True)
    p = jnp.exp(s - m)
    l = jnp.sum(p, axis=1, keepdims=True)
    acc = _dot(p.astype(BF16), vb)

    def body(j, carry):
        m, l, acc = carry
        s, _, vb = logits(j)
        picked = jnp.sum(jnp.where(colb == j, sel, 0.0), axis=1, keepdims=True)
        s = jnp.where(picked > 0.0, s, NEG)
        m_new = jnp.maximum(m, jnp.max(s, axis=1, keepdims=True))
        alpha = jnp.exp(m - m_new)
        p = jnp.exp(s - m_new)
        l = alpha * l + jnp.sum(p, axis=1, keepdims=True)
        acc = alpha * acc + _dot(p.astype(BF16), vb)
        return m_new, l, acc

    m, l, acc = lax.fori_loop(0, own_b, body, (m, l, acc))
    o_ref[0, 0] = acc / l


def moba_prompt(q_hm, k_hm, v_hm, kmean_hm, slopes):
    B, H, T, Dh = q_hm.shape
    nb = kmean_hm.shape[2]
    assert T % MOBA_BLOCK == 0 and Dh == HEAD_DIM
    return pl.pallas_call(
        _moba_prompt_kernel,
        out_shape=jax.ShapeDtypeStruct((B, H, T, Dh), F32),
        grid=(B, H, T // MOBA_BLOCK),
        in_specs=[pl.BlockSpec(memory_space=pltpu.SMEM),
                  pl.BlockSpec((1, 1, MOBA_BLOCK, Dh), lambda b, h, i: (b, h, i, 0)),
                  pl.BlockSpec((1, 1, T, Dh), lambda b, h, i: (b, h, 0, 0)),
                  pl.BlockSpec((1, 1, T, Dh), lambda b, h, i: (b, h, 0, 0)),
                  pl.BlockSpec((1, 1, nb, Dh), lambda b, h, i: (b, h, 0, 0))],
        out_specs=pl.BlockSpec((1, 1, MOBA_BLOCK, Dh), lambda b, h, i: (b, h, i, 0)),
        compiler_params=_params("parallel", "parallel", "arbitrary"),
        name="moba_prompt",
    )(slopes, q_hm, k_hm, v_hm, kmean_hm)


def _tri_later(n):
    r = lax.broadcasted_iota(jnp.int32, (n, n), 0)
    c = lax.broadcasted_iota(jnp.int32, (n, n), 1)
    return (r > c).astype(BF16)


def _sb_chunk(qb, kc, vc, tri, log_surv, acc, valid, scale, transposed=False):
    z = (_dot(qb, kc) if transposed else _dot_nt(qb, kc)) * scale
    ls = _log_sigmoid(z)
    lf = ls - z
    if valid is not None:
        lf = jnp.where(valid, lf, 0.0)
    after = _dot_f32_by_exact(lf, tri)
    a = jnp.exp(ls + after + log_surv)
    if valid is not None:
        a = jnp.where(valid, a, 0.0)
    a = a.astype(BF16)
    acc = acc + (_dot_nt(a, vc) if transposed else _dot(a, vc))
    log_surv = log_surv + jnp.sum(lf, axis=1, keepdims=True)
    return log_surv, acc


def _any_alive(log_surv):
    return jnp.max(log_surv) > SB_DEAD


def _sb_prompt_kernel(q_ref, k_ref, v_ref, o_ref):
    qj = pl.program_id(2)
    scale = HEAD_DIM ** -0.5
    qb = q_ref[0, 0]
    tri = _tri_later(SB_TILE)

    def chunk(c):
        start = pl.multiple_of(c * SB_TILE, SB_TILE)
        return k_ref[0, 0, pl.ds(start, SB_TILE), :], v_ref[0, 0, pl.ds(start, SB_TILE), :]

    r = lax.broadcasted_iota(jnp.int32, (SB_TILE, SB_TILE), 0)
    c = lax.broadcasted_iota(jnp.int32, (SB_TILE, SB_TILE), 1)
    kc, vc = chunk(qj)
    log_surv, acc = _sb_chunk(qb, kc, vc, tri, jnp.zeros((SB_TILE, 1), F32),
                              jnp.zeros((SB_TILE, HEAD_DIM), F32), c < r, scale)

    def cond(carry):
        nxt, log_surv, _ = carry
        return jnp.logical_and(nxt >= 0, _any_alive(log_surv))

    def body(carry):
        nxt, log_surv, acc = carry
        kc, vc = chunk(nxt)
        log_surv, acc = _sb_chunk(qb, kc, vc, tri, log_surv, acc, None, scale)
        return nxt - 1, log_surv, acc

    _, _, acc = lax.while_loop(cond, body, (qj - 1, log_surv, acc))
    o_ref[0, 0] = acc


def sb_prompt(q_hm, k_hm, v_hm):
    B, H, T, Dh = q_hm.shape
    assert T % SB_TILE == 0
    return pl.pallas_call(
        _sb_prompt_kernel,
        out_shape=jax.ShapeDtypeStruct((B, H, T, Dh), F32),
        grid=(B, H, T // SB_TILE),
        in_specs=[pl.BlockSpec((1, 1, SB_TILE, Dh), lambda b, h, i: (b, h, i, 0)),
                  pl.BlockSpec((1, 1, T, Dh), lambda b, h, i: (b, h, 0, 0)),
                  pl.BlockSpec((1, 1, T, Dh), lambda b, h, i: (b, h, 0, 0))],
        out_specs=pl.BlockSpec((1, 1, SB_TILE, Dh), lambda b, h, i: (b, h, i, 0)),
        compiler_params=_params("parallel", "parallel", "arbitrary"),
        name="sb_prompt",
    )(q_hm, k_hm, v_hm)


PAGES_PER_STEP = 8
NEW_ROWS = 8


def _sample_rows(qrep_ref):
    q = qrep_ref[0]
    rows, w = q.shape
    r = lax.broadcasted_iota(jnp.int32, (rows, w), 0)
    c = lax.broadcasted_iota(jnp.int32, (rows, w), 1)
    head_mask = (c // HEAD_DIM) == (r % N_HEADS)
    return jnp.where(head_mask, q, 0.0), head_mask


def _new_page(new_ref):
    x = new_ref[0]
    return jnp.concatenate([x, jnp.zeros((PAGE_SIZE - x.shape[0], x.shape[1]), x.dtype)], axis=0)


def _write_rows_per_query(o_ref, acc, head_mask, n_q):
    accm = jnp.where(head_mask, acc, 0.0)
    for i in range(n_q):
        o_ref[0, i:i + 1, :] = jnp.sum(accm[i * N_HEADS:(i + 1) * N_HEADS], axis=0, keepdims=True)


def _moba_sample_kernel(pt_ref, qrep_ref, knew_ref, vnew_ref, slope_ref, *refs, n_pages, n_q):
    pg = PAGES_PER_STEP
    k_refs, v_refs = refs[:pg], refs[pg:2 * pg]
    o_ref, logit_ref, acc_ref, l_ref = refs[2 * pg:]
    g = pl.program_id(1)
    ng = n_pages // pg
    ppb = MOBA_BLOCK // PAGE_SIZE
    nblk = n_pages // ppb
    rows = n_q * N_HEADS
    scale = HEAD_DIM ** -0.5
    qm, head_mask = _sample_rows(qrep_ref)
    qb = qm.astype(BF16)

    @pl.when(g < ng)
    def _keys():
        for i in range(pg):
            logit_ref[g * pg + i] = _dot(qb, k_refs[i][0].astype(BF16))

    @pl.when(g == ng - 1)
    def _select_and_softmax():
        colb = lax.broadcasted_iota(jnp.int32, (rows, nblk), 1)
        sc = jnp.zeros((rows, nblk), F32)
        for b in range(nblk):
            tot = sum(jnp.sum(logit_ref[b * ppb + j], axis=1, keepdims=True) for j in range(ppb))
            sc = jnp.where(colb == b, tot * (1.0 / MOBA_BLOCK), sc)
        sel = _topk_mask(sc, colb >= 0, MOBA_TOPK)
        slope = slope_ref[...]
        past = n_pages * PAGE_SIZE
        qidx = lax.broadcasted_iota(jnp.int32, (rows, PAGE_SIZE), 0) // N_HEADS
        col = lax.broadcasted_iota(jnp.int32, (rows, PAGE_SIZE), 1)
        s_new = _dot_nt(qb, _new_page(knew_ref).astype(BF16)) * scale - slope * (qidx - col).astype(F32)
        s_new = jnp.where((col <= qidx) & (col < n_q), s_new, NEG)
        m = jnp.max(s_new, axis=1, keepdims=True)
        for p in range(n_pages):
            dist = (past + qidx - (p * PAGE_SIZE + col)).astype(F32)
            s = logit_ref[p] * scale - slope * dist
            s = jnp.where(sel[:, p // ppb:p // ppb + 1] > 0.0, s, NEG)
            logit_ref[p] = s
            m = jnp.maximum(m, jnp.max(s, axis=1, keepdims=True))
        p_new = jnp.exp(s_new - m)
        l = jnp.sum(p_new, axis=1, keepdims=True)
        for p in range(n_pages):
            e = jnp.exp(logit_ref[p] - m)
            logit_ref[p] = e
            l = l + jnp.sum(e, axis=1, keepdims=True)
        l_ref[...] = l
        acc_ref[...] = _dot(p_new.astype(BF16), _new_page(vnew_ref).astype(BF16))

    @pl.when(g >= ng)
    def _values():
        acc = acc_ref[...]
        for i in range(pg):
            page = (g - ng) * pg + i
            acc = acc + _dot_nt(logit_ref[page].astype(BF16), v_refs[i][0].astype(BF16))
        acc_ref[...] = acc

    @pl.when(g == 2 * ng - 1)
    def _finish():
        _write_rows_per_query(o_ref, acc_ref[...] / l_ref[...], head_mask, n_q)


def _page_specs(n_pages, page_of_step, width):
    def spec(i):
        return pl.BlockSpec((1, width, PAGE_SIZE),
                            lambda b, g, pt: (pt[b * n_pages + page_of_step(g, i)], 0, 0))
    return [spec(i) for i in range(PAGES_PER_STEP)]


def pool_pages(cache):
    n_phys, page, h, dh = cache.shape
    return cache.transpose(0, 2, 3, 1).reshape(n_phys, h * dh, page)


def _sample_prep(q, k_new, v_new):
    db, n_q, w = q.shape
    qrep = jnp.repeat(q, N_HEADS, axis=1)
    pad = ((0, 0), (0, NEW_ROWS - n_q), (0, 0))
    return qrep, jnp.pad(k_new, pad), jnp.pad(v_new, pad)


def moba_sample(q, k_new, v_new, pool_k, pool_v, page_table, slopes):
    db, n_q, w = q.shape
    n_pages = page_table.shape[1]
    pg = PAGES_PER_STEP
    assert n_pages % pg == 0 and (n_pages * PAGE_SIZE) % MOBA_BLOCK == 0 and n_q <= NEW_ROWS
    ng = n_pages // pg
    rows = n_q * N_HEADS
    qrep, kn, vn = _sample_prep(q, k_new, v_new)
    slope_rows = jnp.tile(slopes, n_q).reshape(rows, 1)
    per_sample = lambda shape: pl.BlockSpec((1,) + shape, lambda b, g, pt: (b, 0, 0))
    grid_spec = pltpu.PrefetchScalarGridSpec(
        num_scalar_prefetch=1,
        grid=(db, 2 * ng),
        in_specs=[per_sample((rows, w)), per_sample((NEW_ROWS, w)), per_sample((NEW_ROWS, w)),
                  pl.BlockSpec((rows, 1), lambda b, g, pt: (0, 0))]
                 + _page_specs(n_pages, lambda g, i: jnp.minimum(g, ng - 1) * pg + i, w)
                 + _page_specs(n_pages, lambda g, i: jnp.maximum(g - ng, 0) * pg + i, w),
        out_specs=per_sample((n_q, w)),
        scratch_shapes=[pltpu.VMEM((n_pages, rows, PAGE_SIZE), F32),
                        pltpu.VMEM((rows, w), F32),
                        pltpu.VMEM((rows, 1), F32)],
    )
    return pl.pallas_call(
        functools.partial(_moba_sample_kernel, n_pages=n_pages, n_q=n_q),
        out_shape=jax.ShapeDtypeStruct((db, n_q, w), F32),
        grid_spec=grid_spec,
        compiler_params=_params("parallel", "arbitrary"),
        name="moba_sample",
    )(page_table.reshape(-1), qrep, kn, vn, slope_rows, *([pool_k] * pg), *([pool_v] * pg))


SB_GROUP = 2


def _sb_sample_kernel(pt_ref, qrep_ref, knew_ref, vnew_ref, kpool, vpool, o_ref,
                      first_k, first_v, more_k, more_v, first_sem, more_sem, *, n_pages, n_q):
    s = pl.program_id(0)
    n_groups = n_pages // SB_GROUP
    rows = n_q * N_HEADS
    scale = HEAD_DIM ** -0.5
    qm, head_mask = _sample_rows(qrep_ref)
    qb = qm.astype(BF16)
    tri = _tri_later(PAGE_SIZE)

    def group_copies(sample, g, kbuf, vbuf, sems, slot):
        copies = []
        for j in range(SB_GROUP):
            page = pt_ref[sample * n_pages + (n_pages - 1 - (g * SB_GROUP + j))]
            copies.append(pltpu.make_async_copy(kpool.at[page], kbuf.at[slot, j], sems.at[slot, 0, j]))
            copies.append(pltpu.make_async_copy(vpool.at[page], vbuf.at[slot, j], sems.at[slot, 1, j]))
        return copies

    def start(copies):
        for c in copies:
            c.start()

    def wait(copies):
        for c in copies:
            c.wait()

    def walk(kbuf, vbuf, slot, surv, acc):
        for j in range(SB_GROUP):
            surv, acc = _sb_chunk(qb, kbuf[slot, j].astype(BF16), vbuf[slot, j].astype(BF16), tri, surv, acc,
                                  None, scale, transposed=True)
        return surv, acc

    @pl.when(s == 0)
    def _first_sample():
        start(group_copies(0, 0, first_k, first_v, first_sem, 0))

    @pl.when(s + 1 < pl.num_programs(0))
    def _prefetch_next_sample():
        start(group_copies(s + 1, 0, first_k, first_v, first_sem, (s + 1) % 2))

    qidx = lax.broadcasted_iota(jnp.int32, (rows, PAGE_SIZE), 0) // N_HEADS
    col = lax.broadcasted_iota(jnp.int32, (rows, PAGE_SIZE), 1)
    surv, acc = _sb_chunk(qb, _new_page(knew_ref).astype(BF16), _new_page(vnew_ref).astype(BF16), tri,
                          jnp.zeros((rows, 1), F32), jnp.zeros(qm.shape, F32), (col < qidx) & (col < n_q), scale)

    wait(group_copies(s, 0, first_k, first_v, first_sem, s % 2))
    surv, acc = walk(first_k, first_v, s % 2, surv, acc)

    def cond(carry):
        g, _, surv, _ = carry
        return jnp.logical_and(g < n_groups, _any_alive(surv))

    def body(carry):
        g, pending, surv, acc = carry
        slot = g % 2

        @pl.when(pending == 0)
        def _fetch_now():
            start(group_copies(s, g, more_k, more_v, more_sem, slot))

        wait(group_copies(s, g, more_k, more_v, more_sem, slot))
        has_next = g + 1 < n_groups

        @pl.when(has_next)
        def _lookahead():
            start(group_copies(s, g + 1, more_k, more_v, more_sem, 1 - slot))

        surv, acc = walk(more_k, more_v, slot, surv, acc)
        return g + 1, has_next.astype(jnp.int32), surv, acc

    g, pending, surv, acc = lax.while_loop(cond, body, (jnp.int32(1), jnp.int32(0), surv, acc))

    @pl.when(pending == 1)
    def _drain_unused_lookahead():
        wait(group_copies(s, g, more_k, more_v, more_sem, g % 2))

    _write_rows_per_query(o_ref, acc, head_mask, n_q)


def sb_sample(q, k_new, v_new, pool_k, pool_v, page_table):
    db, n_q, w = q.shape
    n_pages = page_table.shape[1]
    assert n_pages % SB_GROUP == 0 and n_q <= NEW_ROWS
    rows = n_q * N_HEADS
    qrep, kn, vn = _sample_prep(q, k_new, v_new)
    per_sample = lambda shape: pl.BlockSpec((1,) + shape, lambda b, pt: (b, 0, 0))
    group_buf = pltpu.VMEM((2, SB_GROUP, w, PAGE_SIZE), F32)
    group_sem = pltpu.SemaphoreType.DMA((2, 2, SB_GROUP))
    grid_spec = pltpu.PrefetchScalarGridSpec(
        num_scalar_prefetch=1,
        grid=(db,),
        in_specs=[per_sample((rows, w)), per_sample((NEW_ROWS, w)), per_sample((NEW_ROWS, w)),
                  pl.BlockSpec(memory_space=pl.ANY), pl.BlockSpec(memory_space=pl.ANY)],
        out_specs=per_sample((n_q, w)),
        scratch_shapes=[group_buf, group_buf, group_buf, group_buf, group_sem, group_sem],
    )
    return pl.pallas_call(
        functools.partial(_sb_sample_kernel, n_pages=n_pages, n_q=n_q),
        out_shape=jax.ShapeDtypeStruct((db, n_q, w), F32),
        grid_spec=grid_spec,
        compiler_params=_params("arbitrary"),
        name="sb_sample",
    )(page_table.reshape(-1), qrep, kn, vn, pool_k, pool_v)


def _merge_mem_kernel(x_ref, oa_ref, ob_ref, ga_ref, gb_ref, mk_ref, mv_ref, wa_ref, wb_ref, wo_ref,
                      wmq_ref, wmo_ref, gq_ref, gf_ref, x2_ref, hb_ref):
    tm = x_ref.shape[0]
    n_mem, m_len, w_mem = mk_ref.shape
    ya = _dot(oa_ref[...], wa_ref[...])
    yb = _dot(ob_ref[...], wb_ref[...])
    mix = jax.nn.sigmoid(ga_ref[...]) * ya + jax.nn.sigmoid(gb_ref[...]) * yb
    x1 = x_ref[...] + _dot(mix.astype(BF16), wo_ref[...])

    qm = _dot(_rmsnorm(x1, gq_ref[...]).astype(BF16), wmq_ref[...]).astype(BF16)
    mk = mk_ref[...].reshape(n_mem * m_len, w_mem).astype(BF16)
    mv = mv_ref[...].reshape(n_mem * m_len, w_mem).astype(BF16)
    if n_mem > 1:
        row_mem = lax.broadcasted_iota(jnp.int32, (tm, n_mem * m_len), 0) // (tm // n_mem)
        key_mem = lax.broadcasted_iota(jnp.int32, (tm, n_mem * m_len), 1) // m_len
        same = row_mem == key_mem
    heads = []
    for hd in range(N_HEADS_MEM):
        sl = slice(hd * HEAD_DIM_MEM, (hd + 1) * HEAD_DIM_MEM)
        lg = _dot_nt(qm[:, sl], mk[:, sl]) * (HEAD_DIM_MEM ** -0.5)
        if n_mem > 1:
            lg = jnp.where(same, lg, NEG)
        p = jnp.exp(lg - jnp.max(lg, axis=1, keepdims=True))
        heads.append(_dot(p.astype(BF16), mv[:, sl]) / jnp.sum(p, axis=1, keepdims=True))
    o = jnp.concatenate(heads, axis=1)
    x2 = x1 + _dot(o.astype(BF16), wmo_ref[...])
    x2_ref[...] = x2
    hb_ref[...] = _rmsnorm(x2, gf_ref[...]).astype(BF16)


def merge_mem(x, oa, ob, ga, gb, mem_k, mem_v, wa, wb, wo, wmq, wmo, g_mem_q, g_ffn, tm):
    n, d = x.shape
    n_mem_total, m_len, w_mem = mem_k.shape
    rows_per_mem = n // n_mem_total
    assert n % tm == 0 and (tm % rows_per_mem == 0 or rows_per_mem % tm == 0)
    if rows_per_mem >= tm:
        gm = 1
        mem_map = lambda i: (i * tm // rows_per_mem, 0, 0)
    else:
        gm = tm // rows_per_mem
        mem_map = lambda i: (i, 0, 0)
    row = lambda width: pl.BlockSpec((tm, width), lambda i: (i, 0))
    full = lambda a: pl.BlockSpec(a.shape, lambda i: (0,) * a.ndim)
    gq, gf = g_mem_q.reshape(1, d), g_ffn.reshape(1, d)
    return pl.pallas_call(
        _merge_mem_kernel,
        out_shape=[jax.ShapeDtypeStruct((n, d), F32), jax.ShapeDtypeStruct((n, d), BF16)],
        grid=(n // tm,),
        in_specs=[row(d), row(oa.shape[1]), row(ob.shape[1]), row(d), row(d),
                  pl.BlockSpec((gm, m_len, w_mem), mem_map), pl.BlockSpec((gm, m_len, w_mem), mem_map),
                  full(wa), full(wb), full(wo), full(wmq), full(wmo), full(gq), full(gf)],
        out_specs=[row(d), row(d)],
        compiler_params=_params("parallel"),
        name="merge_mem",
    )(x, oa, ob, ga, gb, mem_k, mem_v, wa, wb, wo, wmq, wmo, gq, gf)


def _peer_query_kernel(hb_ref, wqt_ref, subk_ref, hbt_ref, s_ref):
    hb = hb_ref[...]
    hbt_ref[...] = jnp.transpose(hb.astype(F32)).astype(BF16)
    qt = _dot_nt(wqt_ref[...], hb)
    for hc in range(subk_ref.shape[0]):
        q_hc = qt[hc * PEER_KEYS:(hc + 1) * PEER_KEYS].astype(BF16)
        s_ref[hc] = _dot(subk_ref[hc], q_hc)


def peer_query(hb, wq_t, subk, tn=512):
    n, d = hb.shape
    assert n % tn == 0 and subk.shape[2] == PEER_KEYS
    return pl.pallas_call(
        _peer_query_kernel,
        out_shape=[jax.ShapeDtypeStruct((d, n), BF16), jax.ShapeDtypeStruct((subk.shape[0], PEER_KEYS, n), F32)],
        grid=(n // tn,),
        in_specs=[pl.BlockSpec((tn, d), lambda i: (i, 0)),
                  pl.BlockSpec(wq_t.shape, lambda i: (0, 0)),
                  pl.BlockSpec(subk.shape, lambda i: (0, 0, 0))],
        out_specs=[pl.BlockSpec((d, tn), lambda i: (0, i)),
                   pl.BlockSpec((subk.shape[0], PEER_KEYS, tn), lambda i: (0, 0, i))],
        compiler_params=_params("parallel"),
        name="peer_query",
    )(hb, wq_t, subk)


def _top_ranks(s, k):
    n_keys = s.shape[0]
    kio = lax.broadcasted_iota(jnp.int32, s.shape, 0).astype(F32)
    work = s
    rank = jnp.full(s.shape, float(k), F32)
    vals = []
    for r in range(k):
        m = jnp.max(work, axis=0, keepdims=True)
        idx = jnp.min(jnp.where(work == m, kio, float(n_keys)), axis=0, keepdims=True)
        hit = kio == idx
        rank = jnp.where(hit, float(r), rank)
        work = jnp.where(hit, -jnp.inf, work)
        vals.append(m)
    return rank, vals


def _route_kernel(s_ref, cnt_ref, e1_ref, rank2_ref, e2_ref, cand_ref):
    k = PEER_TOPK
    s1, s2 = s_ref[0], s_ref[1]
    tn = s1.shape[1]
    rank1, v1 = _top_ranks(s1, k)
    rank2, v2 = _top_ranks(s2, k)
    rio = lax.broadcasted_iota(jnp.int32, (k, tn), 0)
    v2_all = jnp.zeros((k, tn), F32)
    for r in range(k):
        v2_all = jnp.where(rio == r, v2[r], v2_all)
    for r in range(k):
        cand_ref[r * k:(r + 1) * k, :] = v1[r] + v2_all
    cand = cand_ref[...]
    chosen, _ = _top_ranks(cand, k)
    chosen = chosen < float(k)
    top0 = v1[0] + v2[0]
    z = jnp.sum(jnp.where(chosen, jnp.exp(cand - top0), 0.0), axis=0, keepdims=True)
    cnt = jnp.zeros(s1.shape, F32)
    for r in range(k):
        c_r = jnp.sum(jnp.where(chosen[r * k:(r + 1) * k], 1.0, 0.0), axis=0, keepdims=True)
        cnt = jnp.where(rank1 == float(r), c_r, cnt)
    cnt_ref[0] = cnt
    e1_ref[0] = jnp.exp(s1 - v1[0]) / z
    rank2_ref[0] = rank2.astype(BF16)
    e2_ref[0] = jnp.exp(s2 - v2[0]).astype(BF16)


def peer_route(scores, tn=128):
    hc, n_keys, n = scores.shape
    assert n % tn == 0
    out = lambda dt: jax.ShapeDtypeStruct((hc // 2, n_keys, n), dt)
    spec = pl.BlockSpec((1, n_keys, tn), lambda i, h: (h, 0, i))
    return pl.pallas_call(
        _route_kernel,
        out_shape=[out(F32), out(F32), out(BF16), out(BF16)],
        grid=(n // tn, hc // 2),
        in_specs=[pl.BlockSpec((2, n_keys, tn), lambda i, h: (h, 0, i))],
        out_specs=[spec] * 4,
        scratch_shapes=[pltpu.VMEM((PEER_TOPK * PEER_TOPK, tn), F32)],
        compiler_params=_params("parallel", "parallel"),
        name="peer_route",
    )(scores)


EXPERT_TILE = 512
LANE_CHUNK = 128


def _gelu_exact(x):
    return 0.5 * x * (1.0 + lax.erf(x * (2.0 ** -0.5)))


def _peer_dense_kernel(hbt_ref, u_ref, vt_ref, cnt_ref, e1_ref, rank2_ref, e2_ref, x2_ref, gfin_ref,
                       y_ref, acc_ref, wt_ref, act_ref, rank2_s, e2_s):
    e = pl.program_id(1)

    @pl.when(e == 0)
    def _init():
        acc_ref[...] = jnp.zeros(acc_ref.shape, F32)
        rank2_s[...] = rank2_ref[...]
        e2_s[...] = e2_ref[...]

    act_ref[...] = _dot(u_ref[...], hbt_ref[...])
    for ai in range(EXPERT_TILE // PEER_KEYS):
        a = e * (EXPERT_TILE // PEER_KEYS) + ai
        rows = slice(ai * PEER_KEYS, (ai + 1) * PEER_KEYS)
        cnt_rows = [cnt_ref[h, pl.ds(a, 1), :].astype(BF16) for h in range(PEER_HEADS)]
        e1_rows = [e1_ref[h, pl.ds(a, 1), :].astype(BF16) for h in range(PEER_HEADS)]
        for tc in range(act_ref.shape[1] // LANE_CHUNK):
            cols = slice(tc * LANE_CHUNK, (tc + 1) * LANE_CHUNK)
            gate = jnp.zeros((PEER_KEYS, LANE_CHUNK), BF16)
            for h in range(PEER_HEADS):
                chosen = rank2_s[h, :, cols] < cnt_rows[h][:, cols]
                gate = gate + jnp.where(chosen, e2_s[h, :, cols], jnp.zeros((), BF16)) * e1_rows[h][:, cols]
            wt_ref[rows, cols] = gate * _gelu_exact(act_ref[rows, cols]).astype(BF16)
    acc_ref[...] += _dot(vt_ref[...], wt_ref[...])

    @pl.when(e == pl.num_programs(1) - 1)
    def _finish():
        y_ref[...] = _rmsnorm(x2_ref[...] + jnp.transpose(acc_ref[...]), gfin_ref[...])


def peer_dense(hbt, u_bf16, vt_bf16, cnt, e1, rank2, e2, x2, g_final, tn=512):
    d, n = hbt.shape
    n_exp = u_bf16.shape[0]
    assert n % tn == 0 and n_exp % EXPERT_TILE == 0 and n_exp == PEER_KEYS * PEER_KEYS
    route_spec = pl.BlockSpec((PEER_HEADS, PEER_KEYS, tn), lambda i, e: (0, 0, i))
    return pl.pallas_call(
        _peer_dense_kernel,
        out_shape=jax.ShapeDtypeStruct((n, d), F32),
        grid=(n // tn, n_exp // EXPERT_TILE),
        in_specs=[pl.BlockSpec((d, tn), lambda i, e: (0, i)),
                  pl.BlockSpec((EXPERT_TILE, d), lambda i, e: (e, 0)),
                  pl.BlockSpec((d, EXPERT_TILE), lambda i, e: (0, e)),
                  route_spec, route_spec, route_spec, route_spec,
                  pl.BlockSpec((tn, d), lambda i, e: (i, 0)),
                  pl.BlockSpec((1, d), lambda i, e: (0, 0))],
        out_specs=pl.BlockSpec((tn, d), lambda i, e: (i, 0)),
        scratch_shapes=[pltpu.VMEM((d, tn), F32), pltpu.VMEM((EXPERT_TILE, tn), BF16),
                        pltpu.VMEM((EXPERT_TILE, tn), F32),
                        pltpu.VMEM((PEER_HEADS, PEER_KEYS, tn), BF16), pltpu.VMEM((PEER_HEADS, PEER_KEYS, tn), BF16)],
        compiler_params=_params("parallel", "arbitrary"),
        name="peer_dense",
    )(hbt, u_bf16, vt_bf16, cnt, e1, rank2, e2, x2, g_final.reshape(1, d))


def peer_ffn_residual_norm(x2, hb, wq_t, subk, u_bf16, vt_bf16, g_final):
    hbt, scores = peer_query(hb, wq_t, subk)
    cnt, e1, rank2, e2 = peer_route(scores)
    return peer_dense(hbt, u_bf16, vt_bf16, cnt, e1, rank2, e2, x2, g_final)


def kernel(x_prompt, x_sample, cache_mem_k, cache_mem_v, cache_moba_k, cache_moba_v, cache_sb_k, cache_sb_v, page_table, mem_prompt, g_mix, w_in, w_br_moba, w_br_sb, w_out, g_mem_q, g_mem_kv, w_mem_q, w_mem_kv, w_mem_o, g_ffn, w_peer_q, peer_sub_keys, peer_u, peer_v, g_final):
    B, T, D = x_prompt.shape
    DB, QN, _ = x_sample.shape
    n_mem = mem_prompt.shape[1]
    w_mem = N_HEADS_MEM * HEAD_DIM_MEM
    slopes = 2.0 ** (-(8.0 / N_HEADS) * jnp.arange(1, N_HEADS + 1, dtype=F32))
    splits = (W_ATT,) * 6 + (D, D)

    w_in_b = w_in.astype(BF16)
    wa, wb, wo = w_br_moba.astype(BF16), w_br_sb.astype(BF16), w_out.astype(BF16)
    wmq, wmo = w_mem_q.astype(BF16), w_mem_o.astype(BF16)
    wq_t = w_peer_q.T.astype(BF16)
    subk = peer_sub_keys.reshape(2 * PEER_HEADS, PEER_KEYS, -1).astype(BF16)
    u_b = peer_u.astype(BF16)
    vt_b = peer_v.T.astype(BF16)

    qa, ka, va, qb, kb, vb, ga, gb, kmean = norm_matmul(x_prompt.reshape(B * T, D), g_mix, w_in_b, splits, kmean_split=1)
    hm = lambda a: a.reshape(B, T, N_HEADS, HEAD_DIM).transpose(0, 2, 1, 3)
    km = kmean.reshape(B, T // MOBA_BLOCK, N_HEADS, HEAD_DIM).transpose(0, 2, 1, 3)
    oa = moba_prompt(hm(qa), hm(ka).astype(BF16), hm(va).astype(BF16), km, slopes)
    ob = sb_prompt(hm(qb).astype(BF16), hm(kb).astype(BF16), hm(vb).astype(BF16))
    tok = lambda o: o.transpose(0, 2, 1, 3).reshape(B * T, W_ATT).astype(BF16)
    mem_k_p, mem_v_p = norm_matmul(mem_prompt.reshape(B * n_mem, D), g_mem_kv, w_mem_kv.astype(BF16), (w_mem, w_mem))
    x2, hb = merge_mem(x_prompt.reshape(B * T, D), tok(oa), tok(ob), ga, gb,
                       mem_k_p.reshape(B, n_mem, w_mem), mem_v_p.reshape(B, n_mem, w_mem),
                       wa, wb, wo, wmq, wmo, g_mem_q, g_ffn, tm=256)
    y_prompt = peer_ffn_residual_norm(x2, hb, wq_t, subk, u_b, vt_b, g_final).reshape(B, T, D)

    qa_s, ka_s, va_s, qb_s, kb_s, vb_s, ga_s, gb_s = norm_matmul(x_sample.reshape(DB * QN, D), g_mix, w_in_b, splits)
    per_sample = lambda a: a.reshape(DB, QN, W_ATT)
    oa_s = moba_sample(per_sample(qa_s), per_sample(ka_s), per_sample(va_s),
                       pool_pages(cache_moba_k), pool_pages(cache_moba_v), page_table, slopes)
    ob_s = sb_sample(per_sample(qb_s), per_sample(kb_s), per_sample(vb_s),
                     pool_pages(cache_sb_k), pool_pages(cache_sb_v), page_table)
    x2_s, hb_s = merge_mem(x_sample.reshape(DB * QN, D), oa_s.reshape(DB * QN, W_ATT).astype(BF16),
                           ob_s.reshape(DB * QN, W_ATT).astype(BF16), ga_s, gb_s,
                           cache_mem_k.reshape(DB, n_mem, w_mem), cache_mem_v.reshape(DB, n_mem, w_mem),
                           wa, wb, wo, wmq, wmo, g_mem_q, g_ffn, tm=8 * QN)
    y_sample = peer_ffn_residual_norm(x2_s, hb_s, wq_t, subk, u_b, vt_b, g_final).reshape(DB, QN, D)

    heads = lambda a, lead: a.reshape(lead + (N_HEADS, HEAD_DIM))
    mem_heads = lambda a: a.reshape(B, n_mem, N_HEADS_MEM, HEAD_DIM_MEM)
    return (y_prompt, y_sample, mem_heads(mem_k_p), mem_heads(mem_v_p),
            heads(ka, (B, T)), heads(va, (B, T)), heads(kb, (B, T)), heads(vb, (B, T)),
            heads(ka_s, (DB, QN)), heads(va_s, (DB, QN)), heads(kb_s, (DB, QN)), heads(vb_s, (DB, QN)))
```

```python
import functools
import math

import jax
import jax.numpy as jnp
from jax import lax
from jax.experimental import pallas as pl
from jax.experimental.pallas import tpu as pltpu

F32 = jnp.float32
BF16 = jnp.bfloat16

RMS_EPS = 1e-6
MOBA_BLOCK = 256
MOBA_TOPK = 3
Q_BLOCK = 128
PAGE_SIZE = 128
N_HEADS = 8
HEAD_DIM = 64
W_ATT = N_HEADS * HEAD_DIM
N_HEADS_MEM = 4
HEAD_DIM_MEM = 128
PEER_HEADS = 8
PEER_KEYS = 128
PEER_TOPK = 16
NEG = -1e30
SB_DEAD = -105.0
SB_TILE = 256
VMEM_LIMIT = 56 * 1024 * 1024


def _dot(a, b):
    return jnp.dot(a, b, preferred_element_type=F32)


def _dot_nt(a, b):
    return lax.dot_general(a, b, (((1,), (1,)), ((), ())), preferred_element_type=F32)


def _split3(x):
    hi = x.astype(BF16)
    r = x - hi.astype(F32)
    mid = r.astype(BF16)
    lo = (r - mid.astype(F32)).astype(BF16)
    return hi, mid, lo


def _dot_f32_by_exact(x, m):
    hi, mid, lo = _split3(x)
    return _dot(hi, m) + (_dot(mid, m) + _dot(lo, m))


def _dot_nt_precise(a, b):
    a0, a1, a2 = _split3(a)
    b0, b1, b2 = _split3(b)
    small = _dot_nt(a0, b2) + _dot_nt(a2, b0) + _dot_nt(a1, b1)
    return _dot_nt(a0, b0) + ((_dot_nt(a0, b1) + _dot_nt(a1, b0)) + small)


def _log_sigmoid(z):
    return jnp.minimum(z, 0.0) - jnp.log1p(jnp.exp(-jnp.abs(z)))


def _rmsnorm(x, g):
    ms = jnp.mean(x * x, axis=-1, keepdims=True)
    return x * lax.rsqrt(ms + RMS_EPS) * g


def _params(*sem):
    return pltpu.CompilerParams(dimension_semantics=sem, vmem_limit_bytes=VMEM_LIMIT)


def _norm_matmul_kernel(x_ref, g_ref, w_ref, *out_refs, splits, kmean_split):
    h = _rmsnorm(x_ref[...], g_ref[...]).astype(BF16)
    off = 0
    for i, width in enumerate(splits):
        y = _dot(h, w_ref[:, off:off + width])
        out_refs[i][...] = y
        if i == kmean_split:
            out_refs[len(splits)][0] = jnp.mean(y, axis=0, keepdims=True)
        off += width


def norm_matmul(x, g, w_bf16, splits, kmean_split=None, tm=256):
    n, d = x.shape
    assert n % tm == 0 and sum(splits) == w_bf16.shape[1]
    out_shape = [jax.ShapeDtypeStruct((n, s), F32) for s in splits]
    out_specs = [pl.BlockSpec((tm, s), lambda i: (i, 0)) for s in splits]
    if kmean_split is not None:
        assert tm == MOBA_BLOCK
        out_shape.append(jax.ShapeDtypeStruct((n // tm, 1, splits[kmean_split]), F32))
        out_specs.append(pl.BlockSpec((1, 1, splits[kmean_split]), lambda i: (i, 0, 0)))
    return pl.pallas_call(
        functools.partial(_norm_matmul_kernel, splits=tuple(splits), kmean_split=kmean_split),
        out_shape=out_shape,
        grid=(n // tm,),
        in_specs=[pl.BlockSpec((tm, d), lambda i: (i, 0)),
                  pl.BlockSpec((1, d), lambda i: (0, 0)),
                  pl.BlockSpec(w_bf16.shape, lambda i: (0, 0))],
        out_specs=out_specs,
        compiler_params=_params("parallel"),
        name="norm_matmul",
    )(x, g.reshape(1, d), w_bf16)


def _topk_mask(sc, valid, k):
    nbk = sc.shape[1]
    col = lax.broadcasted_iota(jnp.int32, sc.shape, 1).astype(F32)
    work = jnp.where(valid, sc, -jnp.inf)
    sel = jnp.zeros(sc.shape, F32)
    for _ in range(min(k, nbk)):
        m = jnp.max(work, axis=1, keepdims=True)
        idx = jnp.min(jnp.where(work == m, col, float(nbk)), axis=1, keepdims=True)
        hit = col == idx
        sel = jnp.where(hit, 1.0, sel)
        work = jnp.where(hit, -jnp.inf, work)
    return jnp.where(valid, sel, 0.0)


def _moba_prompt_kernel(slope_ref, q_ref, k_ref, v_ref, km_ref, o_ref):
    h = pl.program_id(1)
    own_b = pl.program_id(2)
    slope = slope_ref[h]
    scale = HEAD_DIM ** -0.5
    q32 = q_ref[0, 0]
    qb = q32.astype(BF16)
    km = km_ref[0, 0]
    nb = km.shape[0]
    t0 = own_b * MOBA_BLOCK

    sc = _dot_nt_precise(q32, km)
    colb = lax.broadcasted_iota(jnp.int32, (MOBA_BLOCK, nb), 1)
    sel = _topk_mask(sc, colb < own_b, MOBA_TOPK)

    t = t0 + lax.broadcasted_iota(jnp.int32, (MOBA_BLOCK, 1), 0)

    def logits(j):
        start = pl.multiple_of(j * MOBA_BLOCK, MOBA_BLOCK)
        kb = k_ref[0, 0, pl.ds(start, MOBA_BLOCK), :]
        vb = v_ref[0, 0, pl.ds(start, MOBA_BLOCK), :]
        kpos = start + lax.broadcasted_iota(jnp.int32, (1, MOBA_BLOCK), 1)
        dist = (t - kpos).astype(F32)
        return _dot_nt(qb, kb) * scale - slope * dist, kpos, vb

    s, kpos, vb = logits(own_b)
    s = jnp.where(kpos <= t, s, NEG)
    m = jnp.max(s, axis=1, keepdims=True)
    p = jnp.exp(s - m)
    l = jnp.sum(p, axis=1, keepdims=True)
    acc = _dot(p.astype(BF16), vb)

    def masked(j):
        s, _, vb = logits(j)
        picked = jnp.sum(jnp.where(colb == j, sel, 0.0), axis=1, keepdims=True)
        return jnp.where(picked > 0.0, s, NEG), vb

    def fold(carry, blocks):
        m, l, acc = carry
        m_new = m
        for s, _ in blocks:
            m_new = jnp.maximum(m_new, jnp.max(s, axis=1, keepdims=True))
        alpha = jnp.exp(m - m_new)
        l, acc = alpha * l, alpha * acc
        for s, vb in blocks:
            p = jnp.exp(s - m_new)
            l = l + jnp.sum(p, axis=1, keepdims=True)
            acc = acc + _dot(p.astype(BF16), vb)
        return m_new, l, acc

    carry = lax.fori_loop(0, own_b // 2, lambda i, c: fold(c, [masked(2 * i), masked(2 * i + 1)]), (m, l, acc))
    m, l, acc = lax.fori_loop(own_b - own_b % 2, own_b, lambda j, c: fold(c, [masked(j)]), carry)
    o_ref[0, 0] = acc / l


def moba_prompt(q_hm, k_hm, v_hm, kmean_hm, slopes):
    B, H, T, Dh = q_hm.shape
    nb = kmean_hm.shape[2]
    assert T % MOBA_BLOCK == 0 and Dh == HEAD_DIM
    return pl.pallas_call(
        _moba_prompt_kernel,
        out_shape=jax.ShapeDtypeStruct((B, H, T, Dh), F32),
        grid=(B, H, T // MOBA_BLOCK),
        in_specs=[pl.BlockSpec(memory_space=pltpu.SMEM),
                  pl.BlockSpec((1, 1, MOBA_BLOCK, Dh), lambda b, h, i: (b, h, i, 0)),
                  pl.BlockSpec((1, 1, T, Dh), lambda b, h, i: (b, h, 0, 0)),
                  pl.BlockSpec((1, 1, T, Dh), lambda b, h, i: (b, h, 0, 0)),
                  pl.BlockSpec((1, 1, nb, Dh), lambda b, h, i: (b, h, 0, 0))],
        out_specs=pl.BlockSpec((1, 1, MOBA_BLOCK, Dh), lambda b, h, i: (b, h, i, 0)),
        compiler_params=_params("parallel", "parallel", "arbitrary"),
        name="moba_prompt",
    )(slopes, q_hm, k_hm, v_hm, kmean_hm)


def _tri_later(n):
    r = lax.broadcasted_iota(jnp.int32, (n, n), 0)
    c = lax.broadcasted_iota(jnp.int32, (n, n), 1)
    return (r > c).astype(BF16)


def _sb_chunk(qb, kc, vc, tri, log_surv, acc, valid, scale, transposed=False):
    z = (_dot(qb, kc) if transposed else _dot_nt(qb, kc)) * scale
    ls = _log_sigmoid(z)
    lf = ls - z
    if valid is not None:
        lf = jnp.where(valid, lf, 0.0)
    after = _dot_f32_by_exact(lf, tri)
    a = jnp.exp(ls + after + log_surv)
    if valid is not None:
        a = jnp.where(valid, a, 0.0)
    a = a.astype(BF16)
    acc = acc + (_dot_nt(a, vc) if transposed else _dot(a, vc))
    log_surv = log_surv + jnp.sum(lf, axis=1, keepdims=True)
    return log_surv, acc


def _any_alive(log_surv):
    return jnp.max(log_surv) > SB_DEAD


def _sb_prompt_kernel(q_ref, k_ref, v_ref, o_ref):
    qj = pl.program_id(2)
    scale = HEAD_DIM ** -0.5
    qb = q_ref[0, 0]
    tri = _tri_later(SB_TILE)

    def chunk(c):
        start = pl.multiple_of(c * SB_TILE, SB_TILE)
        return k_ref[0, 0, pl.ds(start, SB_TILE), :], v_ref[0, 0, pl.ds(start, SB_TILE), :]

    r = lax.broadcasted_iota(jnp.int32, (SB_TILE, SB_TILE), 0)
    c = lax.broadcasted_iota(jnp.int32, (SB_TILE, SB_TILE), 1)
    kc, vc = chunk(qj)
    log_surv, acc = _sb_chunk(qb, kc, vc, tri, jnp.zeros((SB_TILE, 1), F32),
                              jnp.zeros((SB_TILE, HEAD_DIM), F32), c < r, scale)

    def cond(carry):
        nxt, log_surv, _ = carry
        return jnp.logical_and(nxt >= 0, _any_alive(log_surv))

    def body(carry):
        nxt, log_surv, acc = carry
        kc, vc = chunk(nxt)
        log_surv, acc = _sb_chunk(qb, kc, vc, tri, log_surv, acc, None, scale)
        return nxt - 1, log_surv, acc

    _, _, acc = lax.while_loop(cond, body, (qj - 1, log_surv, acc))
    o_ref[0, 0] = acc


def sb_prompt(q_hm, k_hm, v_hm):
    B, H, T, Dh = q_hm.shape
    assert T % SB_TILE == 0
    return pl.pallas_call(
        _sb_prompt_kernel,
        out_shape=jax.ShapeDtypeStruct((B, H, T, Dh), F32),
        grid=(B, H, T // SB_TILE),
        in_specs=[pl.BlockSpec((1, 1, SB_TILE, Dh), lambda b, h, i: (b, h, i, 0)),
                  pl.BlockSpec((1, 1, T, Dh), lambda b, h, i: (b, h, 0, 0)),
                  pl.BlockSpec((1, 1, T, Dh), lambda b, h, i: (b, h, 0, 0))],
        out_specs=pl.BlockSpec((1, 1, SB_TILE, Dh), lambda b, h, i: (b, h, i, 0)),
        compiler_params=_params("parallel", "parallel", "arbitrary"),
        name="sb_prompt",
    )(q_hm, k_hm, v_hm)


PAGES_PER_STEP = 32
NEW_ROWS = 8


def _sample_rows(qrep_ref):
    q = qrep_ref[0]
    rows, w = q.shape
    r = lax.broadcasted_iota(jnp.int32, (rows, w), 0)
    c = lax.broadcasted_iota(jnp.int32, (rows, w), 1)
    head_mask = (c // HEAD_DIM) == (r % N_HEADS)
    return jnp.where(head_mask, q, 0.0), head_mask


def _new_page(new_ref):
    x = new_ref[0]
    return jnp.concatenate([x, jnp.zeros((PAGE_SIZE - x.shape[0], x.shape[1]), x.dtype)], axis=0)


def _write_rows_per_query(o_ref, acc, head_mask, n_q):
    accm = jnp.where(head_mask, acc, 0.0)
    for i in range(n_q):
        o_ref[0, i:i + 1, :] = jnp.sum(accm[i * N_HEADS:(i + 1) * N_HEADS], axis=0, keepdims=True)


def _moba_sample_kernel(pt_ref, qrep_ref, knew_ref, vnew_ref, slope_ref, *refs, n_pages, n_q):
    pg = PAGES_PER_STEP
    k_refs, v_refs = refs[:pg], refs[pg:2 * pg]
    o_ref, logit_ref, acc_ref, l_ref = refs[2 * pg:]
    g = pl.program_id(1)
    ng = n_pages // pg
    ppb = MOBA_BLOCK // PAGE_SIZE
    nblk = n_pages // ppb
    rows = n_q * N_HEADS
    scale = HEAD_DIM ** -0.5
    qm, head_mask = _sample_rows(qrep_ref)
    qb = qm.astype(BF16)

    @pl.when(g < ng)
    def _keys():
        for i in range(pg):
            logit_ref[g * pg + i] = _dot(qb, k_refs[i][0].astype(BF16))

    @pl.when(g == ng - 1)
    def _select_and_softmax():
        colb = lax.broadcasted_iota(jnp.int32, (rows, nblk), 1)
        sc = jnp.zeros((rows, nblk), F32)
        for b in range(nblk):
            tot = sum(jnp.sum(logit_ref[b * ppb + j], axis=1, keepdims=True) for j in range(ppb))
            sc = jnp.where(colb == b, tot * (1.0 / MOBA_BLOCK), sc)
        sel = _topk_mask(sc, colb >= 0, MOBA_TOPK)
        slope = slope_ref[...]
        past = n_pages * PAGE_SIZE
        qidx = lax.broadcasted_iota(jnp.int32, (rows, PAGE_SIZE), 0) // N_HEADS
        col = lax.broadcasted_iota(jnp.int32, (rows, PAGE_SIZE), 1)
        s_new = _dot_nt(qb, _new_page(knew_ref).astype(BF16)) * scale - slope * (qidx - col).astype(F32)
        s_new = jnp.where((col <= qidx) & (col < n_q), s_new, NEG)
        m = jnp.max(s_new, axis=1, keepdims=True)
        for p in range(n_pages):
            dist = (past + qidx - (p * PAGE_SIZE + col)).astype(F32)
            s = logit_ref[p] * scale - slope * dist
            s = jnp.where(sel[:, p // ppb:p // ppb + 1] > 0.0, s, NEG)
            logit_ref[p] = s
            m = jnp.maximum(m, jnp.max(s, axis=1, keepdims=True))
        p_new = jnp.exp(s_new - m)
        l = jnp.sum(p_new, axis=1, keepdims=True)
        for p in range(n_pages):
            e = jnp.exp(logit_ref[p] - m)
            logit_ref[p] = e
            l = l + jnp.sum(e, axis=1, keepdims=True)
        l_ref[...] = l
        acc_ref[...] = _dot(p_new.astype(BF16), _new_page(vnew_ref).astype(BF16))

    @pl.when(g >= ng)
    def _values():
        acc = acc_ref[...]
        for i in range(pg):
            page = (g - ng) * pg + i
            acc = acc + _dot_nt(logit_ref[page].astype(BF16), v_refs[i][0].astype(BF16))
        acc_ref[...] = acc

    @pl.when(g == 2 * ng - 1)
    def _finish():
        _write_rows_per_query(o_ref, acc_ref[...] / l_ref[...], head_mask, n_q)


def _page_specs(n_pages, page_of_step, width):
    def spec(i):
        return pl.BlockSpec((1, width, PAGE_SIZE),
                            lambda b, g, pt: (pt[b * n_pages + page_of_step(g, i)], 0, 0))
    return [spec(i) for i in range(PAGES_PER_STEP)]


def pool_pages(cache):
    n_phys, page, h, dh = cache.shape
    return cache.transpose(0, 2, 3, 1).reshape(n_phys, h * dh, page)


def _sample_prep(q, k_new, v_new):
    db, n_q, w = q.shape
    qrep = jnp.repeat(q, N_HEADS, axis=1)
    pad = ((0, 0), (0, NEW_ROWS - n_q), (0, 0))
    return qrep, jnp.pad(k_new, pad), jnp.pad(v_new, pad)


def moba_sample(q, k_new, v_new, pool_k, pool_v, page_table, slopes):
    db, n_q, w = q.shape
    n_pages = page_table.shape[1]
    pg = PAGES_PER_STEP
    assert n_pages % pg == 0 and (n_pages * PAGE_SIZE) % MOBA_BLOCK == 0 and n_q <= NEW_ROWS
    ng = n_pages // pg
    rows = n_q * N_HEADS
    qrep, kn, vn = _sample_prep(q, k_new, v_new)
    slope_rows = jnp.tile(slopes, n_q).reshape(rows, 1)
    per_sample = lambda shape: pl.BlockSpec((1,) + shape, lambda b, g, pt: (b, 0, 0))
    grid_spec = pltpu.PrefetchScalarGridSpec(
        num_scalar_prefetch=1,
        grid=(db, 2 * ng),
        in_specs=[per_sample((rows, w)), per_sample((NEW_ROWS, w)), per_sample((NEW_ROWS, w)),
                  pl.BlockSpec((rows, 1), lambda b, g, pt: (0, 0))]
                 + _page_specs(n_pages, lambda g, i: jnp.minimum(g, ng - 1) * pg + i, w)
                 + _page_specs(n_pages, lambda g, i: jnp.maximum(g - ng, 0) * pg + i, w),
        out_specs=per_sample((n_q, w)),
        scratch_shapes=[pltpu.VMEM((n_pages, rows, PAGE_SIZE), F32),
                        pltpu.VMEM((rows, w), F32),
                        pltpu.VMEM((rows, 1), F32)],
    )
    return pl.pallas_call(
        functools.partial(_moba_sample_kernel, n_pages=n_pages, n_q=n_q),
        out_shape=jax.ShapeDtypeStruct((db, n_q, w), F32),
        grid_spec=grid_spec,
        compiler_params=_params("parallel", "arbitrary"),
        name="moba_sample",
    )(page_table.reshape(-1), qrep, kn, vn, slope_rows, *([pool_k] * pg), *([pool_v] * pg))


SB_GROUP = 2


def _sb_sample_kernel(pt_ref, qrep_ref, knew_ref, vnew_ref, kpool, vpool, o_ref,
                      first_k, first_v, more_k, more_v, first_sem, more_sem, *, n_pages, n_q):
    s = pl.program_id(0)
    n_groups = n_pages // SB_GROUP
    rows = n_q * N_HEADS
    scale = HEAD_DIM ** -0.5
    qm, head_mask = _sample_rows(qrep_ref)
    qb = qm.astype(BF16)
    tri = _tri_later(PAGE_SIZE)

    def group_copies(sample, g, kbuf, vbuf, sems, slot):
        copies = []
        for j in range(SB_GROUP):
            page = pt_ref[sample * n_pages + (n_pages - 1 - (g * SB_GROUP + j))]
            copies.append(pltpu.make_async_copy(kpool.at[page], kbuf.at[slot, j], sems.at[slot, 0, j]))
            copies.append(pltpu.make_async_copy(vpool.at[page], vbuf.at[slot, j], sems.at[slot, 1, j]))
        return copies

    def start(copies):
        for c in copies:
            c.start()

    def wait(copies):
        for c in copies:
            c.wait()

    def walk(kbuf, vbuf, slot, surv, acc):
        for j in range(SB_GROUP):
            surv, acc = _sb_chunk(qb, kbuf[slot, j].astype(BF16), vbuf[slot, j].astype(BF16), tri, surv, acc,
                                  None, scale, transposed=True)
        return surv, acc

    @pl.when(s == 0)
    def _first_sample():
        start(group_copies(0, 0, first_k, first_v, first_sem, 0))

    @pl.when(s + 1 < pl.num_programs(0))
    def _prefetch_next_sample():
        start(group_copies(s + 1, 0, first_k, first_v, first_sem, (s + 1) % 2))

    qidx = lax.broadcasted_iota(jnp.int32, (rows, PAGE_SIZE), 0) // N_HEADS
    col = lax.broadcasted_iota(jnp.int32, (rows, PAGE_SIZE), 1)
    surv, acc = _sb_chunk(qb, _new_page(knew_ref).astype(BF16), _new_page(vnew_ref).astype(BF16), tri,
                          jnp.zeros((rows, 1), F32), jnp.zeros(qm.shape, F32), (col < qidx) & (col < n_q), scale)

    wait(group_copies(s, 0, first_k, first_v, first_sem, s % 2))
    surv, acc = walk(first_k, first_v, s % 2, surv, acc)

    def cond(carry):
        g, _, surv, _ = carry
        return jnp.logical_and(g < n_groups, _any_alive(surv))

    def body(carry):
        g, pending, surv, acc = carry
        slot = g % 2

        @pl.when(pending == 0)
        def _fetch_now():
            start(group_copies(s, g, more_k, more_v, more_sem, slot))

        wait(group_copies(s, g, more_k, more_v, more_sem, slot))
        has_next = g + 1 < n_groups

        @pl.when(has_next)
        def _lookahead():
            start(group_copies(s, g + 1, more_k, more_v, more_sem, 1 - slot))

        surv, acc = walk(more_k, more_v, slot, surv, acc)
        return g + 1, has_next.astype(jnp.int32), surv, acc

    g, pending, surv, acc = lax.while_loop(cond, body, (jnp.int32(1), jnp.int32(0), surv, acc))

    @pl.when(pending == 1)
    def _drain_unused_lookahead():
        wait(group_copies(s, g, more_k, more_v, more_sem, g % 2))

    _write_rows_per_query(o_ref, acc, head_mask, n_q)


def sb_sample(q, k_new, v_new, pool_k, pool_v, page_table):
    db, n_q, w = q.shape
    n_pages = page_table.shape[1]
    assert n_pages % SB_GROUP == 0 and n_q <= NEW_ROWS
    rows = n_q * N_HEADS
    qrep, kn, vn = _sample_prep(q, k_new, v_new)
    per_sample = lambda shape: pl.BlockSpec((1,) + shape, lambda b, pt: (b, 0, 0))
    group_buf = pltpu.VMEM((2, SB_GROUP, w, PAGE_SIZE), F32)
    group_sem = pltpu.SemaphoreType.DMA((2, 2, SB_GROUP))
    grid_spec = pltpu.PrefetchScalarGridSpec(
        num_scalar_prefetch=1,
        grid=(db,),
        in_specs=[per_sample((rows, w)), per_sample((NEW_ROWS, w)), per_sample((NEW_ROWS, w)),
                  pl.BlockSpec(memory_space=pl.ANY), pl.BlockSpec(memory_space=pl.ANY)],
        out_specs=per_sample((n_q, w)),
        scratch_shapes=[group_buf, group_buf, group_buf, group_buf, group_sem, group_sem],
    )
    return pl.pallas_call(
        functools.partial(_sb_sample_kernel, n_pages=n_pages, n_q=n_q),
        out_shape=jax.ShapeDtypeStruct((db, n_q, w), F32),
        grid_spec=grid_spec,
        compiler_params=_params("arbitrary"),
        name="sb_sample",
    )(page_table.reshape(-1), qrep, kn, vn, pool_k, pool_v)


def _merge_mem_kernel(x_ref, oa_ref, ob_ref, ga_ref, gb_ref, mk_ref, mv_ref, wa_ref, wb_ref, wo_ref,
                      wmq_ref, wmo_ref, gq_ref, gf_ref, x2_ref, hb_ref):
    tm = x_ref.shape[0]
    n_mem, m_len, w_mem = mk_ref.shape
    ya = _dot(oa_ref[...], wa_ref[...])
    yb = _dot(ob_ref[...], wb_ref[...])
    mix = jax.nn.sigmoid(ga_ref[...]) * ya + jax.nn.sigmoid(gb_ref[...]) * yb
    x1 = x_ref[...] + _dot(mix.astype(BF16), wo_ref[...])

    qm = _dot(_rmsnorm(x1, gq_ref[...]).astype(BF16), wmq_ref[...]).astype(BF16)
    mk = mk_ref[...].reshape(n_mem * m_len, w_mem).astype(BF16)
    mv = mv_ref[...].reshape(n_mem * m_len, w_mem).astype(BF16)
    if n_mem > 1:
        row_mem = lax.broadcasted_iota(jnp.int32, (tm, n_mem * m_len), 0) // (tm // n_mem)
        key_mem = lax.broadcasted_iota(jnp.int32, (tm, n_mem * m_len), 1) // m_len
        same = row_mem == key_mem
    heads = []
    for hd in range(N_HEADS_MEM):
        sl = slice(hd * HEAD_DIM_MEM, (hd + 1) * HEAD_DIM_MEM)
        lg = _dot_nt(qm[:, sl], mk[:, sl]) * (HEAD_DIM_MEM ** -0.5)
        if n_mem > 1:
            lg = jnp.where(same, lg, NEG)
        p = jnp.exp(lg - jnp.max(lg, axis=1, keepdims=True))
        heads.append(_dot(p.astype(BF16), mv[:, sl]) / jnp.sum(p, axis=1, keepdims=True))
    o = jnp.concatenate(heads, axis=1)
    x2 = x1 + _dot(o.astype(BF16), wmo_ref[...])
    x2_ref[...] = x2
    hb_ref[...] = _rmsnorm(x2, gf_ref[...]).astype(BF16)


def merge_mem(x, oa, ob, ga, gb, mem_k, mem_v, wa, wb, wo, wmq, wmo, g_mem_q, g_ffn, tm):
    n, d = x.shape
    n_mem_total, m_len, w_mem = mem_k.shape
    rows_per_mem = n // n_mem_total
    assert n % tm == 0 and (tm % rows_per_mem == 0 or rows_per_mem % tm == 0)
    if rows_per_mem >= tm:
        gm = 1
        mem_map = lambda i: (i * tm // rows_per_mem, 0, 0)
    else:
        gm = tm // rows_per_mem
        mem_map = lambda i: (i, 0, 0)
    row = lambda width: pl.BlockSpec((tm, width), lambda i: (i, 0))
    full = lambda a: pl.BlockSpec(a.shape, lambda i: (0,) * a.ndim)
    gq, gf = g_mem_q.reshape(1, d), g_ffn.reshape(1, d)
    return pl.pallas_call(
        _merge_mem_kernel,
        out_shape=[jax.ShapeDtypeStruct((n, d), F32), jax.ShapeDtypeStruct((n, d), BF16)],
        grid=(n // tm,),
        in_specs=[row(d), row(oa.shape[1]), row(ob.shape[1]), row(d), row(d),
                  pl.BlockSpec((gm, m_len, w_mem), mem_map), pl.BlockSpec((gm, m_len, w_mem), mem_map),
                  full(wa), full(wb), full(wo), full(wmq), full(wmo), full(gq), full(gf)],
        out_specs=[row(d), row(d)],
        compiler_params=_params("parallel"),
        name="merge_mem",
    )(x, oa, ob, ga, gb, mem_k, mem_v, wa, wb, wo, wmq, wmo, gq, gf)


def _peer_query_kernel(hb_ref, wqt_ref, subk_ref, hbt_ref, s_ref):
    hb = hb_ref[...]
    hbt_ref[...] = jnp.transpose(hb.astype(F32)).astype(BF16)
    qt = _dot_nt(wqt_ref[...], hb)
    for hc in range(subk_ref.shape[0]):
        q_hc = qt[hc * PEER_KEYS:(hc + 1) * PEER_KEYS].astype(BF16)
        s_ref[hc] = _dot(subk_ref[hc], q_hc)


def peer_query(hb, wq_t, subk, tn=512):
    n, d = hb.shape
    assert n % tn == 0 and subk.shape[2] == PEER_KEYS
    return pl.pallas_call(
        _peer_query_kernel,
        out_shape=[jax.ShapeDtypeStruct((d, n), BF16), jax.ShapeDtypeStruct((subk.shape[0], PEER_KEYS, n), F32)],
        grid=(n // tn,),
        in_specs=[pl.BlockSpec((tn, d), lambda i: (i, 0)),
                  pl.BlockSpec(wq_t.shape, lambda i: (0, 0)),
                  pl.BlockSpec(subk.shape, lambda i: (0, 0, 0))],
        out_specs=[pl.BlockSpec((d, tn), lambda i: (0, i)),
                   pl.BlockSpec((subk.shape[0], PEER_KEYS, tn), lambda i: (0, 0, i))],
        compiler_params=_params("parallel"),
        name="peer_query",
    )(hb, wq_t, subk)


SUBLANES = 8


def _cand_groups(k):
    groups, row = [], 0
    for r1 in range(k // 2):
        n_valid = k // (r1 + 1)
        groups.append((r1, row, n_valid))
        row += -(-n_valid // SUBLANES) * SUBLANES
    return groups, row, row + k // 2


def _top_ranks_exact(s, k, pos):
    work = s
    rank = jnp.full(s.shape, float(k), F32)
    vals = []
    for r in range(k):
        m = jnp.max(work, axis=0, keepdims=True)
        idx = jnp.min(jnp.where(work == m, pos, float(2 ** 20)), axis=0, keepdims=True)
        hit = pos == idx
        rank = jnp.where(hit, float(r), rank)
        work = jnp.where(hit, -jnp.inf, work)
        vals.append(m)
    return rank, vals


def _top_ranks_fast(s, k):
    work = s
    rank = jnp.full(s.shape, float(k), F32)
    vals = []
    for r in range(k):
        m = jnp.max(work, axis=0, keepdims=True)
        hit = work == m
        rank = jnp.where(hit, float(r), rank)
        work = jnp.where(hit, -jnp.inf, work)
        vals.append(m)
    return rank, vals


def _n_ranked(rank, k):
    return jnp.sum(jnp.where(rank < float(k), 1.0, 0.0), axis=0, keepdims=True)


def _route_body(s1, s2, cand_ref, exact):
    k = PEER_TOPK
    tn = s1.shape[1]
    groups, shared_row, n_rows = _cand_groups(k)
    pad8 = lambda n: -(-n // SUBLANES) * SUBLANES
    if exact:
        kio = lax.broadcasted_iota(jnp.int32, s1.shape, 0).astype(F32)
        rank1, v1 = _top_ranks_exact(s1, k, kio)
        rank2, v2 = _top_ranks_exact(s2, k, kio)
    else:
        rank1, v1 = _top_ranks_fast(s1, k)
        rank2, v2 = _top_ranks_fast(s2, k)
    rio = lax.broadcasted_iota(jnp.int32, (k, tn), 0)
    v1_all = jnp.zeros((k, tn), F32)
    v2_all = jnp.zeros((k, tn), F32)
    for r in range(k):
        v1_all = jnp.where(rio == r, v1[r], v1_all)
        v2_all = jnp.where(rio == r, v2[r], v2_all)
    for r1, row, n_valid in groups:
        r2 = lax.broadcasted_iota(jnp.int32, (pad8(n_valid), tn), 0)
        cand_ref[row:row + pad8(n_valid), :] = jnp.where(r2 < n_valid, v1[r1] + v2_all[:pad8(n_valid)], -jnp.inf)
    cand_ref[shared_row:n_rows, :] = v1_all[k // 2:] + v2[0]
    cand = cand_ref[...]
    if exact:
        row_io = lax.broadcasted_iota(jnp.int32, (n_rows, tn), 0)
        pos = (k // 2 + row_io - shared_row) * k
        for r1, row, n_valid in groups:
            pos = jnp.where((row_io >= row) & (row_io < row + pad8(n_valid)), r1 * k + (row_io - row), pos)
        crank, _ = _top_ranks_exact(cand, k, pos.astype(F32))
    else:
        crank, _ = _top_ranks_fast(cand, k)
    chosen = crank < float(k)
    top0 = v1[0] + v2[0]
    z = jnp.sum(jnp.where(chosen, jnp.exp(cand - top0), 0.0), axis=0, keepdims=True)
    ones = jnp.where(chosen, 1.0, 0.0)
    cnt = jnp.zeros(s1.shape, F32)
    for r1, row, n_valid in groups:
        c_r = jnp.sum(ones[row:row + pad8(n_valid)], axis=0, keepdims=True)
        cnt = jnp.where(rank1 == float(r1), c_r, cnt)
    for j in range(k // 2):
        cnt = jnp.where(rank1 == float(k // 2 + j), ones[shared_row + j:shared_row + j + 1], cnt)
    outs = (cnt, jnp.exp(s1 - v1[0]) / z, rank2.astype(BF16), jnp.exp(s2 - v2[0]).astype(BF16))
    excess = (jnp.max(jnp.abs(_n_ranked(rank1, k) - k)) + jnp.max(jnp.abs(_n_ranked(rank2, k) - k))
              + jnp.max(jnp.abs(_n_ranked(crank, k) - k)))
    return outs, excess == 0.0


def _route_kernel(s_ref, cnt_ref, e1_ref, rank2_ref, e2_ref, cand_ref):
    out_refs = (cnt_ref, e1_ref, rank2_ref, e2_ref)
    outs, tie_free = _route_body(s_ref[0], s_ref[1], cand_ref, exact=False)
    for ref, val in zip(out_refs, outs):
        ref[0] = val

    @pl.when(jnp.logical_not(tie_free))
    def _redo_with_tie_breaking():
        outs, _ = _route_body(s_ref[0], s_ref[1], cand_ref, exact=True)
        for ref, val in zip(out_refs, outs):
            ref[0] = val


def peer_route(scores, tn=128):
    hc, n_keys, n = scores.shape
    assert n % tn == 0
    out = lambda dt: jax.ShapeDtypeStruct((hc // 2, n_keys, n), dt)
    spec = pl.BlockSpec((1, n_keys, tn), lambda i, h: (h, 0, i))
    return pl.pallas_call(
        _route_kernel,
        out_shape=[out(F32), out(F32), out(BF16), out(BF16)],
        grid=(n // tn, hc // 2),
        in_specs=[pl.BlockSpec((2, n_keys, tn), lambda i, h: (h, 0, i))],
        out_specs=[spec] * 4,
        scratch_shapes=[pltpu.VMEM((_cand_groups(PEER_TOPK)[2], tn), F32)],
        compiler_params=_params("parallel", "parallel"),
        name="peer_route",
    )(scores)


EXPERT_TILE = 1024
LANE_CHUNK = 128


def _gelu_exact(x):
    return 0.5 * x * (1.0 + lax.erf(x * (2.0 ** -0.5)))


def _peer_dense_kernel(hbt_ref, u_ref, vt_ref, cnt_ref, e1_ref, rank2_ref, e2_ref, x2_ref, gfin_ref,
                       y_ref, acc_ref, wt_ref, act_ref, rank2_s, e2_s):
    e = pl.program_id(1)

    @pl.when(e == 0)
    def _init():
        acc_ref[...] = jnp.zeros(acc_ref.shape, F32)
        rank2_s[...] = rank2_ref[...]
        e2_s[...] = e2_ref[...]

    act_ref[...] = _dot(u_ref[...], hbt_ref[...])
    for ai in range(EXPERT_TILE // PEER_KEYS):
        a = e * (EXPERT_TILE // PEER_KEYS) + ai
        rows = slice(ai * PEER_KEYS, (ai + 1) * PEER_KEYS)
        cnt_rows = [cnt_ref[h, pl.ds(a, 1), :].astype(BF16) for h in range(PEER_HEADS)]
        e1_rows = [e1_ref[h, pl.ds(a, 1), :].astype(BF16) for h in range(PEER_HEADS)]
        for tc in range(act_ref.shape[1] // LANE_CHUNK):
            cols = slice(tc * LANE_CHUNK, (tc + 1) * LANE_CHUNK)
            gate = jnp.zeros((PEER_KEYS, LANE_CHUNK), BF16)
            for h in range(PEER_HEADS):
                chosen = rank2_s[h, :, cols] < cnt_rows[h][:, cols]
                gate = gate + jnp.where(chosen, e2_s[h, :, cols], jnp.zeros((), BF16)) * e1_rows[h][:, cols]
            wt_ref[rows, cols] = gate * _gelu_exact(act_ref[rows, cols]).astype(BF16)
    acc_ref[...] += _dot(vt_ref[...], wt_ref[...])

    @pl.when(e == pl.num_programs(1) - 1)
    def _finish():
        y_ref[...] = _rmsnorm(x2_ref[...] + jnp.transpose(acc_ref[...]), gfin_ref[...])


def peer_dense(hbt, u_bf16, vt_bf16, cnt, e1, rank2, e2, x2, g_final, tn=512):
    d, n = hbt.shape
    n_exp = u_bf16.shape[0]
    assert n % tn == 0 and n_exp % EXPERT_TILE == 0 and n_exp == PEER_KEYS * PEER_KEYS
    route_spec = pl.BlockSpec((PEER_HEADS, PEER_KEYS, tn), lambda i, e: (0, 0, i))
    return pl.pallas_call(
        _peer_dense_kernel,
        out_shape=jax.ShapeDtypeStruct((n, d), F32),
        grid=(n // tn, n_exp // EXPERT_TILE),
        in_specs=[pl.BlockSpec((d, tn), lambda i, e: (0, i)),
                  pl.BlockSpec((EXPERT_TILE, d), lambda i, e: (e, 0)),
                  pl.BlockSpec((d, EXPERT_TILE), lambda i, e: (0, e)),
                  route_spec, route_spec, route_spec, route_spec,
                  pl.BlockSpec((tn, d), lambda i, e: (i, 0)),
                  pl.BlockSpec((1, d), lambda i, e: (0, 0))],
        out_specs=pl.BlockSpec((tn, d), lambda i, e: (i, 0)),
        scratch_shapes=[pltpu.VMEM((d, tn), F32), pltpu.VMEM((EXPERT_TILE, tn), BF16),
                        pltpu.VMEM((EXPERT_TILE, tn), F32),
                        pltpu.VMEM((PEER_HEADS, PEER_KEYS, tn), BF16), pltpu.VMEM((PEER_HEADS, PEER_KEYS, tn), BF16)],
        compiler_params=_params("parallel", "arbitrary"),
        name="peer_dense",
    )(hbt, u_bf16, vt_bf16, cnt, e1, rank2, e2, x2, g_final.reshape(1, d))


def peer_ffn_residual_norm(x2, hb, wq_t, subk, u_bf16, vt_bf16, g_final):
    hbt, scores = peer_query(hb, wq_t, subk)
    cnt, e1, rank2, e2 = peer_route(scores)
    return peer_dense(hbt, u_bf16, vt_bf16, cnt, e1, rank2, e2, x2, g_final)


def kernel(x_prompt, x_sample, cache_mem_k, cache_mem_v, cache_moba_k, cache_moba_v, cache_sb_k, cache_sb_v, page_table, mem_prompt, g_mix, w_in, w_br_moba, w_br_sb, w_out, g_mem_q, g_mem_kv, w_mem_q, w_mem_kv, w_mem_o, g_ffn, w_peer_q, peer_sub_keys, peer_u, peer_v, g_final):
    B, T, D = x_prompt.shape
    DB, QN, _ = x_sample.shape
    n_mem = mem_prompt.shape[1]
    w_mem = N_HEADS_MEM * HEAD_DIM_MEM
    slopes = 2.0 ** (-(8.0 / N_HEADS) * jnp.arange(1, N_HEADS + 1, dtype=F32))
    splits = (W_ATT,) * 6 + (D, D)

    w_in_b = w_in.astype(BF16)
    wa, wb, wo = w_br_moba.astype(BF16), w_br_sb.astype(BF16), w_out.astype(BF16)
    wmq, wmo = w_mem_q.astype(BF16), w_mem_o.astype(BF16)
    wq_t = w_peer_q.T.astype(BF16)
    subk = peer_sub_keys.reshape(2 * PEER_HEADS, PEER_KEYS, -1).astype(BF16)
    u_b = peer_u.astype(BF16)
    vt_b = peer_v.T.astype(BF16)

    qa, ka, va, qb, kb, vb, ga, gb, kmean = norm_matmul(x_prompt.reshape(B * T, D), g_mix, w_in_b, splits, kmean_split=1)
    hm = lambda a: a.reshape(B, T, N_HEADS, HEAD_DIM).transpose(0, 2, 1, 3)
    km = kmean.reshape(B, T // MOBA_BLOCK, N_HEADS, HEAD_DIM).transpose(0, 2, 1, 3)
    oa = moba_prompt(hm(qa), hm(ka).astype(BF16), hm(va).astype(BF16), km, slopes)
    ob = sb_prompt(hm(qb).astype(BF16), hm(kb).astype(BF16), hm(vb).astype(BF16))
    tok = lambda o: o.transpose(0, 2, 1, 3).reshape(B * T, W_ATT).astype(BF16)
    mem_k_p, mem_v_p = norm_matmul(mem_prompt.reshape(B * n_mem, D), g_mem_kv, w_mem_kv.astype(BF16), (w_mem, w_mem))
    x2, hb = merge_mem(x_prompt.reshape(B * T, D), tok(oa), tok(ob), ga, gb,
                       mem_k_p.reshape(B, n_mem, w_mem), mem_v_p.reshape(B, n_mem, w_mem),
                       wa, wb, wo, wmq, wmo, g_mem_q, g_ffn, tm=256)
    y_prompt = peer_ffn_residual_norm(x2, hb, wq_t, subk, u_b, vt_b, g_final).reshape(B, T, D)

    qa_s, ka_s, va_s, qb_s, kb_s, vb_s, ga_s, gb_s = norm_matmul(x_sample.reshape(DB * QN, D), g_mix, w_in_b, splits)
    per_sample = lambda a: a.reshape(DB, QN, W_ATT)
    oa_s = moba_sample(per_sample(qa_s), per_sample(ka_s), per_sample(va_s),
                       pool_pages(cache_moba_k), pool_pages(cache_moba_v), page_table, slopes)
    ob_s = sb_sample(per_sample(qb_s), per_sample(kb_s), per_sample(vb_s),
                     pool_pages(cache_sb_k), pool_pages(cache_sb_v), page_table)
    x2_s, hb_s = merge_mem(x_sample.reshape(DB * QN, D), oa_s.reshape(DB * QN, W_ATT).astype(BF16),
                           ob_s.reshape(DB * QN, W_ATT).astype(BF16), ga_s, gb_s,
                           cache_mem_k.reshape(DB, n_mem, w_mem), cache_mem_v.reshape(DB, n_mem, w_mem),
                           wa, wb, wo, wmq, wmo, g_mem_q, g_ffn, tm=8 * QN)
    y_sample = peer_ffn_residual_norm(x2_s, hb_s, wq_t, subk, u_b, vt_b, g_final).reshape(DB, QN, D)

    heads = lambda a, lead: a.reshape(lead + (N_HEADS, HEAD_DIM))
    mem_heads = lambda a: a.reshape(B, n_mem, N_HEADS_MEM, HEAD_DIM_MEM)
    return (y_prompt, y_sample, mem_heads(mem_k_p), mem_heads(mem_v_p),
            heads(ka, (B, T)), heads(va, (B, T)), heads(kb, (B, T)), heads(vb, (B, T)),
            heads(ka_s, (DB, QN)), heads(va_s, (DB, QN)), heads(kb_s, (DB, QN)), heads(vb_s, (DB, QN)))
```
